```python
import jax, jax.numpy as jnp
from jax import lax
import numpy as np

D_MODEL = 2048
BATCH = 8
SEQ = 2048
DEPTH = 2

N_MIXERS = 2
GLA_HEADS = 4
GLA_DK = D_MODEL // 2
GLA_DV = D_MODEL
GLA_DK_HEAD = GLA_DK // GLA_HEADS
GLA_DV_HEAD = GLA_DV // GLA_HEADS
GLA_RANK = 16
GLA_TAU = 16.0
GLA_CHUNK = 64
FOX_HEADS = 16
FOX_HEAD_DIM = D_MODEL // FOX_HEADS
FOX_Q_BLOCK = 128
D_FF = 5632
CONV_WIDTH = 3
NORM_EPS = 1e-6
MOD_SCALE = 0.1

kernel_name = "hybrid_gla_fox_convffn_adaln"


def rmsnorm(x, gain):
    xf = x.astype(jnp.float32)
    xf = xf * lax.rsqrt(jnp.mean(xf * xf, axis=-1, keepdims=True) + NORM_EPS)
    return xf.astype(x.dtype) * gain


def gla_mixer(h, w_in, w_gate, b_gate, g_norm, w_out):
    bsz, seq, _ = h.shape
    n_chunks = seq // GLA_CHUNK
    proj = h @ w_in
    q, k, v, r, a = jnp.split(
        proj, [GLA_DK, 2 * GLA_DK, 2 * GLA_DK + GLA_DV, 2 * GLA_DK + 2 * GLA_DV], axis=-1)
    log_alpha = jax.nn.log_sigmoid((a @ w_gate + b_gate).astype(jnp.float32)) / GLA_TAU

    def chunks(t):
        return t.astype(jnp.float32).reshape(
            bsz, n_chunks, GLA_CHUNK, GLA_HEADS, -1).transpose(0, 3, 1, 2, 4)

    q = chunks(q) * (GLA_DK_HEAD ** -0.5)
    k = chunks(k)
    v = chunks(v)
    b = jnp.cumsum(chunks(log_alpha), axis=3)

    q_dec = q * jnp.exp(b)
    k_inv = k * jnp.exp(-b)
    causal = jnp.tril(jnp.ones((GLA_CHUNK, GLA_CHUNK), dtype=bool))
    attn = jnp.where(causal, jnp.einsum('bhnck,bhnsk->bhncs', q_dec, k_inv), 0.0)
    o_intra = jnp.einsum('bhncs,bhnsv->bhncv', attn, v)

    b_last = b[:, :, :, -1:, :]
    k_end = k * jnp.exp(b_last - b)
    chunk_decay = jnp.exp(b_last[:, :, :, 0, :])

    def step(state, xs):
        qd, kd, vv, dec = xs
        o = jnp.einsum('bhck,bhkv->bhcv', qd, state)
        state = dec[..., None] * state + jnp.einsum('bhck,bhcv->bhkv', kd, vv)
        return state, o

    state0 = jnp.zeros((bsz, GLA_HEADS, GLA_DK_HEAD, GLA_DV_HEAD), jnp.float32)
    xs = tuple(jnp.moveaxis(t, 2, 0) for t in (q_dec, k_end, v, chunk_decay))
    _, o_inter = lax.scan(step, state0, xs)
    o = o_intra + jnp.moveaxis(o_inter, 0, 2)
    o = o.transpose(0, 2, 3, 1, 4).reshape(bsz, seq, GLA_HEADS, GLA_DV_HEAD)
    o = rmsnorm(o, g_norm.reshape(GLA_HEADS, GLA_DV_HEAD).astype(jnp.float32))
    o = o.reshape(bsz, seq, GLA_DV).astype(h.dtype) * jax.nn.silu(r)
    return o @ w_out


def fox_mixer(h, w_in, b_f, q_gain, k_gain, w_out):
    bsz, seq, _ = h.shape
    proj = h @ w_in
    q, k, v, og, fl = jnp.split(proj, [D_MODEL, 2 * D_MODEL, 3 * D_MODEL, 4 * D_MODEL], axis=-1)
    q = rmsnorm(q.reshape(bsz, seq, FOX_HEADS, FOX_HEAD_DIM), q_gain) * (FOX_HEAD_DIM ** -0.5)
    k = rmsnorm(k.reshape(bsz, seq, FOX_HEADS, FOX_HEAD_DIM), k_gain)
    v = v.reshape(bsz, seq, FOX_HEADS, FOX_HEAD_DIM)
    q, k, v = (t.transpose(0, 2, 1, 3) for t in (q, k, v))
    log_f = jax.nn.log_sigmoid((fl + b_f).astype(jnp.float32))
    cum = jnp.cumsum(log_f, axis=1).transpose(0, 2, 1)

    n_blocks = seq // FOX_Q_BLOCK
    q_blocks = q.reshape(bsz, FOX_HEADS, n_blocks, FOX_Q_BLOCK, FOX_HEAD_DIM).transpose(2, 0, 1, 3, 4)
    cum_blocks = cum.reshape(bsz, FOX_HEADS, n_blocks, FOX_Q_BLOCK).transpose(2, 0, 1, 3)
    key_pos = jnp.arange(seq)

    def attend(args):
        qb, cb, blk = args
        q_pos = blk * FOX_Q_BLOCK + jnp.arange(FOX_Q_BLOCK)
        logits = (jnp.einsum('bhqd,bhkd->bhqk', qb, k).astype(jnp.float32)
                  + cb[..., None] - cum[:, :, None, :])
        logits = jnp.where(key_pos[None, :] <= q_pos[:, None], logits, -jnp.inf)
        p = jax.nn.softmax(logits, axis=-1)
        return jnp.einsum('bhqk,bhkd->bhqd', p.astype(v.dtype), v)

    o = lax.map(attend, (q_blocks, cum_blocks, jnp.arange(n_blocks)))
    o = o.transpose(1, 0, 3, 2, 4).reshape(bsz, seq, D_MODEL)
    o = o * jax.nn.sigmoid(og)
    return o @ w_out


def conv_ffn(h, w_up, conv_w, conv_b, w_down):
    u = h @ w_up
    u = lax.conv_general_dilated(
        u, conv_w[:, None, :].astype(u.dtype), window_strides=(1,),
        padding=[(CONV_WIDTH - 1, 0)], dimension_numbers=('NWC', 'WIO', 'NWC'),
        feature_group_count=2 * D_FF) + conv_b
    gate, val = jnp.split(u, 2, axis=-1)
    return (jax.nn.silu(gate) * val) @ w_down


def setup_inputs(seed: int = 0) -> dict:
    key = jax.random.key(seed)
    ks = jax.random.split(key, 24)
    n_gla = (DEPTH + 1) // 2
    n_fox = DEPTH // 2
    f32 = jnp.float32

    def w(k, shape, fan_in, scale=1.0):
        return (scale * fan_in ** -0.5) * jax.random.normal(k, shape, f32)

    def gain(k, shape):
        return 1.0 + 0.05 * jax.random.normal(k, shape, f32)

    gla_in = 2 * GLA_DK + 2 * GLA_DV + GLA_RANK
    fox_in = 4 * D_MODEL + FOX_HEADS
    return {
        "x": jax.random.normal(ks[0], (BATCH, SEQ, D_MODEL), f32),
        "c": jax.random.normal(ks[1], (BATCH, D_MODEL), f32),
        "w_mod": w(ks[2], (DEPTH, D_MODEL, 6 * D_MODEL), D_MODEL, MOD_SCALE),
        "b_mod": 0.02 * jax.random.normal(ks[3], (DEPTH, 6 * D_MODEL), f32),
        "norm_mix": gain(ks[4], (DEPTH, D_MODEL)),
        "norm_ffn": gain(ks[5], (DEPTH, D_MODEL)),
        "gla_w_in": w(ks[6], (n_gla, D_MODEL, gla_in), D_MODEL),
        "gla_w_gate": w(ks[7], (n_gla, GLA_RANK, GLA_DK), GLA_RANK),
        "gla_b_gate": 0.1 * jax.random.normal(ks[8], (n_gla, GLA_DK), f32),
        "gla_norm": gain(ks[9], (n_gla, GLA_DV)),
        "gla_w_out": w(ks[10], (n_gla, GLA_DV, D_MODEL), GLA_DV),
        "fox_w_in": w(ks[11], (n_fox, D_MODEL, fox_in), D_MODEL),
        "fox_b_f": 3.0 + 0.5 * jax.random.normal(ks[12], (n_fox, FOX_HEADS), f32),
        "fox_q_norm": gain(ks[13], (n_fox, FOX_HEAD_DIM)),
        "fox_k_norm": gain(ks[14], (n_fox, FOX_HEAD_DIM)),
        "fox_w_out": w(ks[15], (n_fox, D_MODEL, D_MODEL), D_MODEL),
        "ffn_w_up": w(ks[16], (DEPTH, D_MODEL, 2 * D_FF), D_MODEL),
        "ffn_conv_w": w(ks[17], (DEPTH, CONV_WIDTH, 2 * D_FF), CONV_WIDTH),
        "ffn_conv_b": 0.02 * jax.random.normal(ks[18], (DEPTH, 2 * D_FF), f32),
        "ffn_w_down": w(ks[19], (DEPTH, D_FF, D_MODEL), D_FF),
        "norm_final": gain(ks[20], (D_MODEL,)),
    }


def reference(x, c, w_mod, b_mod, norm_mix, norm_ffn,
              gla_w_in, gla_w_gate, gla_b_gate, gla_norm, gla_w_out,
              fox_w_in, fox_b_f, fox_q_norm, fox_k_norm, fox_w_out,
              ffn_w_up, ffn_conv_w, ffn_conv_b, ffn_w_down, norm_final):
    cond = jax.nn.silu(c)
    for i in range(DEPTH):
        mod = (cond @ w_mod[i] + b_mod[i])[:, None, :]
        sh_m, sc_m, g_m, sh_f, sc_f, g_f = jnp.split(mod, 6, axis=-1)
        h = rmsnorm(x, norm_mix[i]) * (1.0 + sc_m) + sh_m
        j = i // N_MIXERS
        if i % N_MIXERS == 0:
            y = gla_mixer(h, gla_w_in[j], gla_w_gate[j], gla_b_gate[j], gla_norm[j], gla_w_out[j])
        else:
            y = fox_mixer(h, fox_w_in[j], fox_b_f[j], fox_q_norm[j], fox_k_norm[j], fox_w_out[j])
        x = x + (1.0 + g_m) * y
        h = rmsnorm(x, norm_ffn[i]) * (1.0 + sc_f) + sh_f
        x = x + (1.0 + g_f) * conv_ffn(h, ffn_w_up[i], ffn_conv_w[i], ffn_conv_b[i], ffn_w_down[i])
    return rmsnorm(x, norm_final)
```

```python
import functools

import jax
import jax.numpy as jnp
from jax import lax
from jax.experimental import pallas as pl
from jax.experimental.pallas import tpu as pltpu

F32 = jnp.float32
BF16 = jnp.bfloat16

NORM_EPS = 1e-6
GLA_HEADS = 4
GLA_RANK = 16
GLA_TAU = 16.0
GLA_CHUNK = 64
FOX_HEADS = 16
CONV_WIDTH = 3

LANES = 128
SUBLANES = 8
V7X_VMEM_BYTES = 64 * 1024 * 1024
VMEM_LIMIT_BYTES = V7X_VMEM_BYTES - 8 * 1024 * 1024

NT_DIMS = (((1,), (1,)), ((), ()))
TN_DIMS = (((0,), (0,)), ((), ()))


def _params(*semantics):
    return pltpu.CompilerParams(dimension_semantics=semantics, vmem_limit_bytes=VMEM_LIMIT_BYTES)


def _dot(a, b):
    return jnp.dot(a, b, preferred_element_type=F32)


def _rms_scale(x):
    return x * lax.rsqrt(jnp.mean(x * x, axis=-1, keepdims=True) + NORM_EPS)


def _log_sigmoid(x):
    return jnp.minimum(x, 0.0) - jnp.log1p(jnp.exp(-jnp.abs(x)))


def _split_bf16(x, parts):
    out = []
    for _ in range(parts):
        hi = x.astype(BF16)
        out.append(hi)
        x = x - hi.astype(F32)
    return out


def _mod_kernel(c_ref, w_ref, b_ref, o_ref):
    c = c_ref[...]
    cond = c * jax.nn.sigmoid(c)
    o_ref[0] = _dot(cond.astype(BF16), w_ref[0].astype(BF16)) + b_ref[0]


def _modulation(c, w_mod, b_mod, *, tn=1024):
    depth, d, n = w_mod.shape
    bsz = c.shape[0]
    return pl.pallas_call(
        _mod_kernel,
        grid=(depth, n // tn),
        in_specs=[
            pl.BlockSpec((bsz, d), lambda l, j: (0, 0)),
            pl.BlockSpec((1, d, tn), lambda l, j: (l, 0, j)),
            pl.BlockSpec((1, 1, tn), lambda l, j: (l, 0, j)),
        ],
        out_specs=pl.BlockSpec((1, bsz, tn), lambda l, j: (l, 0, j)),
        out_shape=jax.ShapeDtypeStruct((depth, bsz, n), F32),
        compiler_params=_params("parallel", "parallel"),
        name="adaln_modulation",
    )(c, w_mod, b_mod.reshape(depth, 1, n))


def _norm_proj_kernel(x_ref, gain_ref, sc_ref, sh_ref, w_ref, ws_ref, o_ref, os_ref, h_ref):
    @pl.when(pl.program_id(1) == 0)
    def _():
        h = _rms_scale(x_ref[...]) * gain_ref[...] * (1.0 + sc_ref[0]) + sh_ref[0]
        h_ref[...] = h.astype(BF16)
        os_ref[...] = _dot(h_ref[...], ws_ref[...])

    o_ref[...] = _dot(h_ref[...], w_ref[...]).astype(o_ref.dtype)


def _norm_proj(x, gain, sc, sh, w, w_small, *, seq, tm=1024, tn=1024):
    m, d = x.shape
    n = w.shape[1]
    tiles_per_seq = seq // tm
    return pl.pallas_call(
        _norm_proj_kernel,
        grid=(m // tm, n // tn),
        in_specs=[
            pl.BlockSpec((tm, d), lambda i, j: (i, 0)),
            pl.BlockSpec((1, d), lambda i, j: (0, 0)),
            pl.BlockSpec((1, 1, d), lambda i, j: (i // tiles_per_seq, 0, 0)),
            pl.BlockSpec((1, 1, d), lambda i, j: (i // tiles_per_seq, 0, 0)),
            pl.BlockSpec((d, tn), lambda i, j: (0, j)),
            pl.BlockSpec((d, LANES), lambda i, j: (0, 0)),
        ],
        out_specs=[
            pl.BlockSpec((tm, tn), lambda i, j: (i, j)),
            pl.BlockSpec((tm, LANES), lambda i, j: (i, 0)),
        ],
        out_shape=[
            jax.ShapeDtypeStruct((m, n), BF16),
            jax.ShapeDtypeStruct((m, LANES), F32),
        ],
        scratch_shapes=[pltpu.VMEM((tm, d), BF16)],
        compiler_params=_params("parallel", "arbitrary"),
        name="norm_modulate_in_proj",
    )(x, gain, sc, sh, w, w_small)


def _proj_residual_kernel(a_ref, w_ref, x_ref, g_ref, o_ref):
    o_ref[...] = x_ref[...] + (1.0 + g_ref[0]) * _dot(a_ref[...], w_ref[...])


def _proj_residual(a, w, x, g, *, seq, tm=1024, tn=1024):
    m, k = a.shape
    n = w.shape[1]
    tiles_per_seq = seq // tm
    return pl.pallas_call(
        _proj_residual_kernel,
        grid=(m // tm, n // tn),
        in_specs=[
            pl.BlockSpec((tm, k), lambda i, j: (i, 0)),
            pl.BlockSpec((k, tn), lambda i, j: (0, j)),
            pl.BlockSpec((tm, tn), lambda i, j: (i, j)),
            pl.BlockSpec((1, 1, tn), lambda i, j: (i // tiles_per_seq, 0, j)),
        ],
        out_specs=pl.BlockSpec((tm, tn), lambda i, j: (i, j)),
        out_shape=jax.ShapeDtypeStruct((m, n), F32),
        compiler_params=_params("parallel", "parallel"),
        name="out_proj_residual",
    )(a, w, x, g)


def _gla_kernel(q_ref, k_ref, v_ref, r_ref, a_ref, wg_ref, bg_ref, gn_ref, o_ref, state_ref, *, chunk):
    @pl.when(pl.program_id(2) == 0)
    def _():
        state_ref[...] = jnp.zeros_like(state_ref)

    rows, dk = q_ref.shape
    q_scale = dk ** -0.5
    log_alpha = _log_sigmoid(_dot(a_ref[...].astype(BF16), wg_ref[...]) + bg_ref[...]) / GLA_TAU

    row = lax.broadcasted_iota(jnp.int32, (chunk, chunk), 0)
    col = lax.broadcasted_iota(jnp.int32, (chunk, chunk), 1)
    causal = row >= col
    tri = causal.astype(BF16)

    for c in range(rows // chunk):
        sl = pl.ds(c * chunk, chunk)
        la = log_alpha[c * chunk:(c + 1) * chunk]
        b = sum(_dot(tri, part) for part in _split_bf16(la, 2))
        b_last = b[chunk - 1:chunk]
        q = q_ref[sl, :].astype(F32)
        k = k_ref[sl, :].astype(F32)
        v = v_ref[sl, :]
        q_dec = (q * (q_scale * jnp.exp(b))).astype(BF16)
        k_inv = (k * jnp.exp(-b)).astype(BF16)
        k_end = (k * jnp.exp(b_last - b)).astype(BF16)

        attn = jnp.where(causal, lax.dot_general(q_dec, k_inv, NT_DIMS, preferred_element_type=F32), 0.0)
        state = state_ref[...]
        o = _dot(attn.astype(BF16), v)
        o = o + lax.dot_general(q_dec, state.astype(BF16), NT_DIMS, preferred_element_type=F32)
        state_ref[...] = state * jnp.exp(b_last) + lax.dot_general(v, k_end, TN_DIMS, preferred_element_type=F32)

        r = r_ref[sl, :].astype(F32)
        o = (_rms_scale(o) * gn_ref[...]) * (r * jax.nn.sigmoid(r))
        o_ref[sl, :] = o.astype(o_ref.dtype)


def _gla_core(proj, a, wg, bg, gn, *, bsz, seq, rows=512):
    m = proj.shape[0]
    dk = wg.shape[1] // GLA_HEADS
    dv = gn.shape[1] // GLA_HEADS
    nblk = seq // rows
    k_off = (GLA_HEADS * dk) // dk
    v_off = (2 * GLA_HEADS * dk) // dv
    r_off = v_off + GLA_HEADS

    def row_blk(b, h, t):
        return b * nblk + t

    return pl.pallas_call(
        functools.partial(_gla_kernel, chunk=GLA_CHUNK),
        grid=(bsz, GLA_HEADS, nblk),
        in_specs=[
            pl.BlockSpec((rows, dk), lambda b, h, t: (row_blk(b, h, t), h)),
            pl.BlockSpec((rows, dk), lambda b, h, t: (row_blk(b, h, t), k_off + h)),
            pl.BlockSpec((rows, dv), lambda b, h, t: (row_blk(b, h, t), v_off + h)),
            pl.BlockSpec((rows, dv), lambda b, h, t: (row_blk(b, h, t), r_off + h)),
            pl.BlockSpec((rows, LANES), lambda b, h, t: (row_blk(b, h, t), 0)),
            pl.BlockSpec((LANES, dk), lambda b, h, t: (0, h)),
            pl.BlockSpec((1, dk), lambda b, h, t: (0, h)),
            pl.BlockSpec((1, dv), lambda b, h, t: (0, h)),
        ],
        out_specs=pl.BlockSpec((rows, dv), lambda b, h, t: (row_blk(b, h, t), h)),
        out_shape=jax.ShapeDtypeStruct((m, GLA_HEADS * dv), BF16),
        scratch_shapes=[pltpu.VMEM((dv, dk), F32)],
        compiler_params=_params("parallel", "parallel", "arbitrary"),
        name="gla_chunked",
    )(proj, proj, proj, proj, a, wg, bg, gn)


def _fox_cum_kernel(fl_ref, bf_ref, o_ref, *, blk):
    seq = fl_ref.shape[0]
    log_f = _log_sigmoid(fl_ref[...] + bf_ref[...])
    row = lax.broadcasted_iota(jnp.int32, (blk, blk), 0)
    col = lax.broadcasted_iota(jnp.int32, (blk, blk), 1)
    tri = (row >= col).astype(BF16)
    carry = jnp.zeros((1, LANES), F32)
    pieces = []
    for s in range(seq // blk):
        lf = log_f[s * blk:(s + 1) * blk]
        cum = sum(_dot(tri, part) for part in _split_bf16(lf, 3)) + carry
        carry = cum[blk - 1:blk]
        pieces.append(cum)
    cum = jnp.concatenate(pieces, axis=0)
    o_ref[0] = cum.T[:o_ref.shape[1]]


def _fox_cum(fl, bf, *, bsz, seq, blk=256):
    return pl.pallas_call(
        functools.partial(_fox_cum_kernel, blk=blk),
        grid=(bsz,),
        in_specs=[
            pl.BlockSpec((seq, LANES), lambda b: (b, 0)),
            pl.BlockSpec((1, LANES), lambda b: (0, 0)),
        ],
        out_specs=pl.BlockSpec((1, FOX_HEADS, seq), lambda b: (b, 0, 0)),
        out_shape=jax.ShapeDtypeStruct((bsz, FOX_HEADS, seq), F32),
        compiler_params=_params("parallel"),
        name="fox_cum_log_forget",
    )(fl, bf)


def _fox_attn_kernel(q_ref, k_ref, v_ref, og_ref, ck_ref, qg_ref, kg_ref, o_ref, kn_ref, *, tk):
    qi = pl.program_id(2)
    tq, hd = q_ref.shape

    @pl.when(qi == 0)
    def _():
        kn_ref[...] = (_rms_scale(k_ref[...].astype(F32)) * kg_ref[...]).astype(BF16)

    qn = ((_rms_scale(q_ref[...].astype(F32)) * qg_ref[...]) * (hd ** -0.5)).astype(BF16)
    q_pos = qi * tq + lax.broadcasted_iota(jnp.int32, (tq, tk), 0)
    k_iota = lax.broadcasted_iota(jnp.int32, (tq, tk), 1)

    def body(j, carry):
        m_prev, l_prev, acc = carry
        off = pl.multiple_of(j * tk, tk)
        kb = kn_ref[pl.ds(off, tk), :]
        vb = v_ref[pl.ds(off, tk), :]
        s = lax.dot_general(qn, kb, NT_DIMS, preferred_element_type=F32) - ck_ref[0, 0, :, pl.ds(off, tk)]
        s = jnp.where(k_iota + off <= q_pos, s, -jnp.inf)
        m_new = jnp.maximum(m_prev, jnp.max(s, axis=-1, keepdims=True))
        alpha = jnp.exp(m_prev - m_new)
        p = jnp.exp(s - m_new)
        l_new = alpha * l_prev + jnp.sum(p, axis=-1, keepdims=True)
        acc = alpha * acc + _dot(p.astype(BF16), vb)
        return m_new, l_new, acc

    n_kv = (qi * tq + tq + tk - 1) // tk
    init = (jnp.full((tq, 1), -jnp.inf, F32), jnp.zeros((tq, 1), F32), jnp.zeros((tq, hd), F32))
    _, l, acc = lax.fori_loop(0, n_kv, body, init)
    og = og_ref[...].astype(F32)
    o_ref[...] = ((acc / l) * jax.nn.sigmoid(og)).astype(o_ref.dtype)


def _fox_attention(proj, ck, qg, kg, *, bsz, seq, tq=256, tk=256):
    m = proj.shape[0]
    hd = qg.shape[1]
    nq = seq // tq
    return pl.pallas_call(
        functools.partial(_fox_attn_kernel, tk=tk),
        grid=(bsz, FOX_HEADS, nq),
        in_specs=[
            pl.BlockSpec((tq, hd), lambda b, h, i: (b * nq + i, h)),
            pl.BlockSpec((seq, hd), lambda b, h, i: (b, FOX_HEADS + h)),
            pl.BlockSpec((seq, hd), lambda b, h, i: (b, 2 * FOX_HEADS + h)),
            pl.BlockSpec((tq, hd), lambda b, h, i: (b * nq + i, 3 * FOX_HEADS + h)),
            pl.BlockSpec((1, 1, 1, seq), lambda b, h, i: (b, h, 0, 0)),
            pl.BlockSpec((1, hd), lambda b, h, i: (0, 0)),
            pl.BlockSpec((1, hd), lambda b, h, i: (0, 0)),
        ],
        out_specs=pl.BlockSpec((tq, hd), lambda b, h, i: (b * nq + i, h)),
        out_shape=jax.ShapeDtypeStruct((m, FOX_HEADS * hd), BF16),
        scratch_shapes=[pltpu.VMEM((seq, hd), BF16)],
        compiler_params=_params("parallel", "parallel", "arbitrary"),
        name="fox_attention",
    )(proj, proj, proj, proj, ck, qg, kg)


def _shift_rows(u, prev, s):
    rolled = pltpu.roll(u, s, 0)
    head_rows = lax.broadcasted_iota(jnp.int32, prev.shape, 0)
    top = jnp.where(head_rows < s, pltpu.roll(prev, s, 0), rolled[:SUBLANES])
    return jnp.concatenate([top, rolled[SUBLANES:]], axis=0)


def _ffn_kernel(x_ref, gain_ref, sc_ref, sh_ref, g_ref, wg_ref, wv_ref, cwg_ref, cwv_ref, cbg_ref, cbv_ref,
                wd_ref, *rest, tiles_per_seq, final_norm):
    if final_norm:
        fin_ref, o_ref, h_ref, acc_ref, carry_ref = rest
    else:
        o_ref, h_ref, acc_ref, carry_ref = rest
    i = pl.program_id(0)
    j = pl.program_id(1)
    tm = x_ref.shape[0]

    @pl.when(j == 0)
    def _():
        h = _rms_scale(x_ref[...]) * gain_ref[...] * (1.0 + sc_ref[0]) + sh_ref[0]
        h_ref[...] = h.astype(BF16)
        acc_ref[...] = jnp.zeros_like(acc_ref)

    @pl.when(i % tiles_per_seq == 0)
    def _():
        carry_ref[j] = jnp.zeros(carry_ref.shape[1:], F32)

    h = h_ref[...]

    def conv(u, slot, cw_ref, cb_ref):
        prev = carry_ref[j, slot]
        carry_ref[j, slot] = u[tm - SUBLANES:]
        out = u * cw_ref[CONV_WIDTH - 1:CONV_WIDTH] + cb_ref[...]
        for s in range(1, CONV_WIDTH):
            out = out + _shift_rows(u, prev, s) * cw_ref[CONV_WIDTH - 1 - s:CONV_WIDTH - s]
        return out

    gate = conv(_dot(h, wg_ref[...]), 0, cwg_ref, cbg_ref)
    val = conv(_dot(h, wv_ref[...]), 1, cwv_ref, cbv_ref)
    act = (gate * jax.nn.sigmoid(gate)) * val
    acc_ref[...] += _dot(act.astype(BF16), wd_ref[...])

    @pl.when(j == pl.num_programs(1) - 1)
    def _():
        y = x_ref[...] + (1.0 + g_ref[0]) * acc_ref[...]
        if final_norm:
            y = _rms_scale(y) * fin_ref[...]
        o_ref[...] = y


def _conv_ffn(x, gain, sc, sh, g, w_up, conv_w, conv_b, w_down, final_gain, *, seq, tm=512, tf=512):
    m, d = x.shape
    d_ff = w_down.shape[0]
    nj = d_ff // tf
    tiles_per_seq = seq // tm
    final_norm = final_gain is not None

    def per_seq(i, j):
        return (i // tiles_per_seq, 0, 0)

    in_specs = [
        pl.BlockSpec((tm, d), lambda i, j: (i, 0)),
        pl.BlockSpec((1, d), lambda i, j: (0, 0)),
        pl.BlockSpec((1, 1, d), per_seq),
        pl.BlockSpec((1, 1, d), per_seq),
        pl.BlockSpec((1, 1, d), per_seq),
        pl.BlockSpec((d, tf), lambda i, j: (0, j)),
        pl.BlockSpec((d, tf), lambda i, j: (0, nj + j)),
        pl.BlockSpec((CONV_WIDTH, tf), lambda i, j: (0, j)),
        pl.BlockSpec((CONV_WIDTH, tf), lambda i, j: (0, nj + j)),
        pl.BlockSpec((1, tf), lambda i, j: (0, j)),
        pl.BlockSpec((1, tf), lambda i, j: (0, nj + j)),
        pl.BlockSpec((tf, d), lambda i, j: (j, 0)),
    ]
    args = [x, gain, sc, sh, g, w_up, w_up, conv_w, conv_w, conv_b, conv_b, w_down]
    if final_norm:
        in_specs.append(pl.BlockSpec((1, d), lambda i, j: (0, 0)))
        args.append(final_gain)
    return pl.pallas_call(
        functools.partial(_ffn_kernel, tiles_per_seq=tiles_per_seq, final_norm=final_norm),
        grid=(m // tm, nj),
        in_specs=in_specs,
        out_specs=pl.BlockSpec((tm, d), lambda i, j: (i, 0)),
        out_shape=jax.ShapeDtypeStruct((m, d), F32),
        scratch_shapes=[
            pltpu.VMEM((tm, d), BF16),
            pltpu.VMEM((tm, d), F32),
            pltpu.VMEM((nj, 2, SUBLANES, tf), F32),
        ],
        compiler_params=_params("arbitrary", "arbitrary"),
        name="conv_ffn",
    )(*args)


def _pad_cols(w, width):
    return jnp.pad(w, ((0, 0), (0, width - w.shape[1])))


def kernel(x, c, w_mod, b_mod, norm_mix, norm_ffn, gla_w_in, gla_w_gate, gla_b_gate, gla_norm, gla_w_out,
           fox_w_in, fox_b_f, fox_q_norm, fox_k_norm, fox_w_out, ffn_w_up, ffn_conv_w, ffn_conv_b,
           ffn_w_down, norm_final):
    bsz, seq, d = x.shape
    depth = w_mod.shape[0]
    m = bsz * seq
    xf = x.reshape(m, d)

    mod = _modulation(c, w_mod, b_mod).reshape(depth, bsz, 6, 1, d)

    for i in range(depth):
        sh_m, sc_m, g_m, sh_f, sc_f, g_f = (mod[i, :, t] for t in range(6))
        j = i // 2
        if i % 2 == 0:
            n_main = gla_w_in.shape[2] - GLA_RANK
            w_in = gla_w_in[j]
            proj, a = _norm_proj(xf, norm_mix[i][None], sc_m, sh_m, w_in[:, :n_main].astype(BF16),
                                 _pad_cols(w_in[:, n_main:], LANES).astype(BF16), seq=seq)
            wg = jnp.pad(gla_w_gate[j], ((0, LANES - GLA_RANK), (0, 0))).astype(BF16)
            o = _gla_core(proj, a, wg, gla_b_gate[j][None], gla_norm[j][None], bsz=bsz, seq=seq)
            w_out = gla_w_out[j]
        else:
            n_main = fox_w_in.shape[2] - FOX_HEADS
            w_in = fox_w_in[j]
            proj, fl = _norm_proj(xf, norm_mix[i][None], sc_m, sh_m, w_in[:, :n_main].astype(BF16),
                                  _pad_cols(w_in[:, n_main:], LANES).astype(BF16), seq=seq)
            ck = _fox_cum(fl, _pad_cols(fox_b_f[j][None], LANES), bsz=bsz, seq=seq)
            o = _fox_attention(proj, ck.reshape(bsz, FOX_HEADS, 1, seq), fox_q_norm[j][None],
                               fox_k_norm[j][None], bsz=bsz, seq=seq)
            w_out = fox_w_out[j]
        xf = _proj_residual(o, w_out.astype(BF16), xf, g_m, seq=seq)
        xf = _conv_ffn(xf, norm_ffn[i][None], sc_f, sh_f, g_f, ffn_w_up[i].astype(BF16), ffn_conv_w[i],
                       ffn_conv_b[i][None], ffn_w_down[i].astype(BF16),
                       norm_final[None] if i == depth - 1 else None, seq=seq)
    return xf.reshape(bsz, seq, d)
```

```python
import functools

import jax
import jax.numpy as jnp
from jax import lax
from jax.experimental import pallas as pl
from jax.experimental.pallas import tpu as pltpu

F32 = jnp.float32
BF16 = jnp.bfloat16

NORM_EPS = 1e-6
GLA_HEADS = 4
GLA_RANK = 16
GLA_TAU = 16.0
GLA_CHUNK = 64
FOX_HEADS = 16
CONV_WIDTH = 3

LANES = 128
SUBLANES = 8
V7X_VMEM_BYTES = 64 * 1024 * 1024
VMEM_LIMIT_BYTES = V7X_VMEM_BYTES - 8 * 1024 * 1024

LOG2_E = 1.4426950408889634

NT_DIMS = (((1,), (1,)), ((), ()))
TN_DIMS = (((0,), (0,)), ((), ()))


def _params(*semantics):
    return pltpu.CompilerParams(dimension_semantics=semantics, vmem_limit_bytes=VMEM_LIMIT_BYTES)


def _dot(a, b):
    return jnp.dot(a, b, preferred_element_type=F32)


def _rms_scale(x):
    return x * lax.rsqrt(jnp.mean(x * x, axis=-1, keepdims=True) + NORM_EPS)


def _log_sigmoid(x):
    return jnp.minimum(x, 0.0) - jnp.log1p(jnp.exp(-jnp.abs(x)))


def _split_bf16(x, parts):
    out = []
    for _ in range(parts):
        hi = x.astype(BF16)
        out.append(hi)
        x = x - hi.astype(F32)
    return out


def _mod_kernel(c_ref, w_ref, b_ref, o_ref):
    c = c_ref[...]
    cond = c * jax.nn.sigmoid(c)
    o_ref[0] = _dot(cond.astype(BF16), w_ref[0].astype(BF16)) + b_ref[0]


def _modulation(c, w_mod, b_mod, *, tn=1024):
    depth, d, n = w_mod.shape
    bsz = c.shape[0]
    return pl.pallas_call(
        _mod_kernel,
        grid=(depth, n // tn),
        in_specs=[
            pl.BlockSpec((bsz, d), lambda l, j: (0, 0)),
            pl.BlockSpec((1, d, tn), lambda l, j: (l, 0, j)),
            pl.BlockSpec((1, 1, tn), lambda l, j: (l, 0, j)),
        ],
        out_specs=pl.BlockSpec((1, bsz, tn), lambda l, j: (l, 0, j)),
        out_shape=jax.ShapeDtypeStruct((depth, bsz, n), F32),
        compiler_params=_params("parallel", "parallel"),
        name="adaln_modulation",
    )(c, w_mod, b_mod.reshape(depth, 1, n))


def _norm_proj_kernel(x_ref, gain_ref, sc_ref, sh_ref, w_ref, ws_ref, o_ref, os_ref, h_ref):
    @pl.when(pl.program_id(1) == 0)
    def _():
        h = _rms_scale(x_ref[...]) * gain_ref[...] * (1.0 + sc_ref[0]) + sh_ref[0]
        h_ref[...] = h.astype(BF16)
        os_ref[...] = _dot(h_ref[...], ws_ref[...])

    o_ref[...] = _dot(h_ref[...], w_ref[...]).astype(o_ref.dtype)


def _norm_proj(x, gain, sc, sh, w, w_small, *, seq, tm=1024, tn=1024):
    m, d = x.shape
    n = w.shape[1]
    tiles_per_seq = seq // tm
    return pl.pallas_call(
        _norm_proj_kernel,
        grid=(m // tm, n // tn),
        in_specs=[
            pl.BlockSpec((tm, d), lambda i, j: (i, 0)),
            pl.BlockSpec((1, d), lambda i, j: (0, 0)),
            pl.BlockSpec((1, 1, d), lambda i, j: (i // tiles_per_seq, 0, 0)),
            pl.BlockSpec((1, 1, d), lambda i, j: (i // tiles_per_seq, 0, 0)),
            pl.BlockSpec((d, tn), lambda i, j: (0, j)),
            pl.BlockSpec((d, LANES), lambda i, j: (0, 0)),
        ],
        out_specs=[
            pl.BlockSpec((tm, tn), lambda i, j: (i, j)),
            pl.BlockSpec((tm, LANES), lambda i, j: (i, 0)),
        ],
        out_shape=[
            jax.ShapeDtypeStruct((m, n), BF16),
            jax.ShapeDtypeStruct((m, LANES), F32),
        ],
        scratch_shapes=[pltpu.VMEM((tm, d), BF16)],
        compiler_params=_params("parallel", "arbitrary"),
        name="norm_modulate_in_proj",
    )(x, gain, sc, sh, w, w_small)


def _proj_residual_kernel(a_ref, w_ref, x_ref, g_ref, o_ref):
    o_ref[...] = x_ref[...] + (1.0 + g_ref[0]) * _dot(a_ref[...], w_ref[...])


def _proj_residual(a, w, x, g, *, seq, tm=1024, tn=1024):
    m, k = a.shape
    n = w.shape[1]
    tiles_per_seq = seq // tm
    return pl.pallas_call(
        _proj_residual_kernel,
        grid=(m // tm, n // tn),
        in_specs=[
            pl.BlockSpec((tm, k), lambda i, j: (i, 0)),
            pl.BlockSpec((k, tn), lambda i, j: (0, j)),
            pl.BlockSpec((tm, tn), lambda i, j: (i, j)),
            pl.BlockSpec((1, 1, tn), lambda i, j: (i // tiles_per_seq, 0, j)),
        ],
        out_specs=pl.BlockSpec((tm, tn), lambda i, j: (i, j)),
        out_shape=jax.ShapeDtypeStruct((m, n), F32),
        compiler_params=_params("parallel", "parallel"),
        name="out_proj_residual",
    )(a, w, x, g)


def _gla_kernel(q_ref, k_ref, v_ref, r_ref, a_ref, wg_ref, bg_ref, gn_ref, o_ref, state_ref, *, chunk):
    @pl.when(pl.program_id(2) == 0)
    def _():
        state_ref[...] = jnp.zeros_like(state_ref)

    rows, dk = q_ref.shape
    q_scale = dk ** -0.5
    log_alpha = _log_sigmoid(_dot(a_ref[...].astype(BF16), wg_ref[...]) + bg_ref[...]) / GLA_TAU

    row = lax.broadcasted_iota(jnp.int32, (chunk, chunk), 0)
    col = lax.broadcasted_iota(jnp.int32, (chunk, chunk), 1)
    causal = row >= col
    tri = causal.astype(BF16)

    for c in range(rows // chunk):
        sl = pl.ds(c * chunk, chunk)
        la = log_alpha[c * chunk:(c + 1) * chunk]
        b = sum(_dot(tri, part) for part in _split_bf16(la, 2))
        b_last = b[chunk - 1:chunk]
        q = q_ref[sl, :].astype(F32)
        k = k_ref[sl, :].astype(F32)
        v = v_ref[sl, :]
        q_dec = (q * (q_scale * jnp.exp(b))).astype(BF16)
        k_inv = (k * jnp.exp(-b)).astype(BF16)
        k_end = (k * jnp.exp(b_last - b)).astype(BF16)

        attn = jnp.where(causal, lax.dot_general(q_dec, k_inv, NT_DIMS, preferred_element_type=F32), 0.0)
        state = state_ref[...]
        o = _dot(attn.astype(BF16), v)
        o = o + lax.dot_general(q_dec, state.astype(BF16), NT_DIMS, preferred_element_type=F32)
        state_ref[...] = state * jnp.exp(b_last) + lax.dot_general(v, k_end, TN_DIMS, preferred_element_type=F32)

        r = r_ref[sl, :].astype(F32)
        o = (_rms_scale(o) * gn_ref[...]) * (r * jax.nn.sigmoid(r))
        o_ref[sl, :] = o.astype(o_ref.dtype)


def _gla_core(proj, a, wg, bg, gn, *, bsz, seq, rows=512):
    m = proj.shape[0]
    dk = wg.shape[1] // GLA_HEADS
    dv = gn.shape[1] // GLA_HEADS
    nblk = seq // rows
    k_off = (GLA_HEADS * dk) // dk
    v_off = (2 * GLA_HEADS * dk) // dv
    r_off = v_off + GLA_HEADS

    def row_blk(b, h, t):
        return b * nblk + t

    return pl.pallas_call(
        functools.partial(_gla_kernel, chunk=GLA_CHUNK),
        grid=(bsz, GLA_HEADS, nblk),
        in_specs=[
            pl.BlockSpec((rows, dk), lambda b, h, t: (row_blk(b, h, t), h)),
            pl.BlockSpec((rows, dk), lambda b, h, t: (row_blk(b, h, t), k_off + h)),
            pl.BlockSpec((rows, dv), lambda b, h, t: (row_blk(b, h, t), v_off + h)),
            pl.BlockSpec((rows, dv), lambda b, h, t: (row_blk(b, h, t), r_off + h)),
            pl.BlockSpec((rows, LANES), lambda b, h, t: (row_blk(b, h, t), 0)),
            pl.BlockSpec((LANES, dk), lambda b, h, t: (0, h)),
            pl.BlockSpec((1, dk), lambda b, h, t: (0, h)),
            pl.BlockSpec((1, dv), lambda b, h, t: (0, h)),
        ],
        out_specs=pl.BlockSpec((rows, dv), lambda b, h, t: (row_blk(b, h, t), h)),
        out_shape=jax.ShapeDtypeStruct((m, GLA_HEADS * dv), BF16),
        scratch_shapes=[pltpu.VMEM((dv, dk), F32)],
        compiler_params=_params("parallel", "parallel", "arbitrary"),
        name="gla_chunked",
    )(proj, proj, proj, proj, a, wg, bg, gn)


def _fox_cum_kernel(fl_ref, bf_ref, o_ref, *, blk):
    seq = fl_ref.shape[0]
    log_f = _log_sigmoid(fl_ref[...] + bf_ref[...])
    row = lax.broadcasted_iota(jnp.int32, (blk, blk), 0)
    col = lax.broadcasted_iota(jnp.int32, (blk, blk), 1)
    tri = (row >= col).astype(BF16)
    carry = jnp.zeros((1, LANES), F32)
    pieces = []
    for s in range(seq // blk):
        lf = log_f[s * blk:(s + 1) * blk]
        cum = sum(_dot(tri, part) for part in _split_bf16(lf, 3)) + carry
        carry = cum[blk - 1:blk]
        pieces.append(cum)
    cum = jnp.concatenate(pieces, axis=0)
    o_ref[0] = cum.T[:o_ref.shape[1]]


def _fox_cum(fl, bf, *, bsz, seq, blk=256):
    return pl.pallas_call(
        functools.partial(_fox_cum_kernel, blk=blk),
        grid=(bsz,),
        in_specs=[
            pl.BlockSpec((seq, LANES), lambda b: (b, 0)),
            pl.BlockSpec((1, LANES), lambda b: (0, 0)),
        ],
        out_specs=pl.BlockSpec((1, FOX_HEADS, seq), lambda b: (b, 0, 0)),
        out_shape=jax.ShapeDtypeStruct((bsz, FOX_HEADS, seq), F32),
        compiler_params=_params("parallel"),
        name="fox_cum_log_forget",
    )(fl, bf)


def _fox_attn_kernel(q_ref, k_ref, v_ref, og_ref, ck_ref, qg_ref, kg_ref, o_ref, kn_ref, qn_ref, m_ref, l_ref,
                     acc_ref, *, heads, hd, rs, tk):
    qi = pl.program_id(2)
    tq = q_ref.shape[0]

    @pl.when(qi == 0)
    def _():
        for g in range(heads):
            cols = slice(g * hd, (g + 1) * hd)
            kn_ref[:, cols] = (_rms_scale(k_ref[:, cols].astype(F32)) * kg_ref[...]).astype(BF16)

    m_ref[...] = jnp.full(m_ref.shape, -jnp.inf, F32)
    l_ref[...] = jnp.zeros(l_ref.shape, F32)
    acc_ref[...] = jnp.zeros(acc_ref.shape, F32)

    q_scale = LOG2_E * hd ** -0.5
    for g in range(heads):
        cols = slice(g * hd, (g + 1) * hd)
        qn_ref[:, cols] = ((_rms_scale(q_ref[:, cols].astype(F32)) * qg_ref[...]) * q_scale).astype(BF16)

    def scores(unit):
        g, r, off, _ = unit
        cols = slice(g * hd, (g + 1) * hd)
        kb = kn_ref[pl.ds(off, tk), cols]
        ckb = ck_ref[0, g, :, pl.ds(off, tk)] * LOG2_E
        return lax.dot_general(qn_ref[r * rs:(r + 1) * rs, cols], kb, NT_DIMS, preferred_element_type=F32) - ckb

    def softmax_update(unit, s):
        g, r, _, diag_shift = unit
        rows = slice(r * rs, (r + 1) * rs)
        if diag_shift is not None:
            visible = (lax.broadcasted_iota(jnp.int32, (rs, tk), 1) + diag_shift
                       <= lax.broadcasted_iota(jnp.int32, (rs, tk), 0))
            s = jnp.where(visible, s, -jnp.inf)
        m_prev = m_ref[g, rows]
        m_new = jnp.maximum(m_prev, jnp.max(s, axis=-1, keepdims=True))
        alpha = jnp.exp2(m_prev - m_new)
        p = jnp.exp2(s - pltpu.repeat(m_new, tk // LANES, axis=1))
        l_ref[g, rows] = alpha * l_ref[g, rows] + jnp.sum(p, axis=-1, keepdims=True)
        m_ref[g, rows] = m_new
        return p.astype(BF16), alpha

    def weighted_values(unit, p, alpha):
        g, r, off, _ = unit
        rows = slice(r * rs, (r + 1) * rs)
        vb = v_ref[pl.ds(off, tk), g * hd:(g + 1) * hd]
        acc_ref[g, rows] = alpha * acc_ref[g, rows] + _dot(p, vb)

    def run(units, lookahead=1):
        s = {u: scores(units[u]) for u in range(min(lookahead, len(units)))}
        for u, unit in enumerate(units):
            p, alpha = softmax_update(unit, s.pop(u))
            if u + lookahead < len(units):
                s[u + lookahead] = scores(units[u + lookahead])
            weighted_values(unit, p, alpha)

    def body(j, carry):
        off = pl.multiple_of(j * tk, tk)
        run([(g, r, off, None) for g in range(heads) for r in range(tq // rs)])
        return carry

    lax.fori_loop(0, qi * (tq // tk), body, 0)

    diag_units = []
    for c in range(tq // tk):
        for g in range(heads):
            for r in range(tq // rs):
                first_key, first_row = c * tk, r * rs
                if first_key > first_row + rs - 1:
                    continue
                unmasked = first_key + tk - 1 <= first_row
                diag_units.append((g, r, pl.multiple_of(qi * tq + first_key, tk),
                                   None if unmasked else first_key - first_row))
    run(diag_units)

    for g in range(heads):
        cols = slice(g * hd, (g + 1) * hd)
        og = og_ref[:, cols].astype(F32)
        o_ref[:, cols] = ((acc_ref[g] / l_ref[g]) * jax.nn.sigmoid(og)).astype(o_ref.dtype)


def _fox_attention(proj, ck, qg, kg, *, bsz, seq, tq=512, heads=4, rs=128, tk=256):
    m = proj.shape[0]
    hd = qg.shape[1]
    assert hd == LANES, "softmax statistics are kept lane-replicated at the head width"
    nq = seq // tq
    groups = FOX_HEADS // heads
    w = heads * hd
    return pl.pallas_call(
        functools.partial(_fox_attn_kernel, heads=heads, hd=hd, rs=rs, tk=tk),
        grid=(bsz, groups, nq),
        in_specs=[
            pl.BlockSpec((tq, w), lambda b, h, i: (b * nq + i, h)),
            pl.BlockSpec((seq, w), lambda b, h, i: (b, groups + h)),
            pl.BlockSpec((seq, w), lambda b, h, i: (b, 2 * groups + h)),
            pl.BlockSpec((tq, w), lambda b, h, i: (b * nq + i, 3 * groups + h)),
            pl.BlockSpec((1, heads, 1, seq), lambda b, h, i: (b, h, 0, 0)),
            pl.BlockSpec((1, hd), lambda b, h, i: (0, 0)),
            pl.BlockSpec((1, hd), lambda b, h, i: (0, 0)),
        ],
        out_specs=pl.BlockSpec((tq, w), lambda b, h, i: (b * nq + i, h)),
        out_shape=jax.ShapeDtypeStruct((m, FOX_HEADS * hd), BF16),
        scratch_shapes=[
            pltpu.VMEM((seq, w), BF16),
            pltpu.VMEM((tq, w), BF16),
            pltpu.VMEM((heads, tq, LANES), F32),
            pltpu.VMEM((heads, tq, LANES), F32),
            pltpu.VMEM((heads, tq, hd), F32),
        ],
        compiler_params=_params("parallel", "parallel", "arbitrary"),
        name="fox_attention",
    )(proj, proj, proj, proj, ck, qg, kg)


def _shift_rows(u, prev, s):
    rolled = pltpu.roll(u, s, 0)
    head_rows = lax.broadcasted_iota(jnp.int32, prev.shape, 0)
    top = jnp.where(head_rows < s, pltpu.roll(prev, s, 0), rolled[:SUBLANES])
    return jnp.concatenate([top, rolled[SUBLANES:]], axis=0)


def _ffn_kernel(x_ref, gain_ref, sc_ref, sh_ref, g_ref, wg_ref, wv_ref, cwg_ref, cwv_ref, cbg_ref, cbv_ref,
                wd_ref, *rest, tiles_per_seq, final_norm):
    if final_norm:
        fin_ref, o_ref, h_ref, acc_ref, carry_ref = rest
    else:
        o_ref, h_ref, acc_ref, carry_ref = rest
    i = pl.program_id(0)
    j = pl.program_id(1)
    tm = x_ref.shape[0]

    @pl.when(j == 0)
    def _():
        h = _rms_scale(x_ref[...]) * gain_ref[...] * (1.0 + sc_ref[0]) + sh_ref[0]
        h_ref[...] = h.astype(BF16)
        acc_ref[...] = jnp.zeros_like(acc_ref)

    @pl.when(i % tiles_per_seq == 0)
    def _():
        carry_ref[j] = jnp.zeros(carry_ref.shape[1:], F32)

    h = h_ref[...]

    def conv(u, slot, cw_ref, cb_ref):
        prev = carry_ref[j, slot]
        carry_ref[j, slot] = u[tm - SUBLANES:]
        out = u * cw_ref[CONV_WIDTH - 1:CONV_WIDTH] + cb_ref[...]
        for s in range(1, CONV_WIDTH):
            out = out + _shift_rows(u, prev, s) * cw_ref[CONV_WIDTH - 1 - s:CONV_WIDTH - s]
        return out

    gate = conv(_dot(h, wg_ref[...]), 0, cwg_ref, cbg_ref)
    val = conv(_dot(h, wv_ref[...]), 1, cwv_ref, cbv_ref)
    act = (gate * jax.nn.sigmoid(gate)) * val
    acc_ref[...] += _dot(act.astype(BF16), wd_ref[...])

    @pl.when(j == pl.num_programs(1) - 1)
    def _():
        y = x_ref[...] + (1.0 + g_ref[0]) * acc_ref[...]
        if final_norm:
            y = _rms_scale(y) * fin_ref[...]
        o_ref[...] = y


def _conv_ffn(x, gain, sc, sh, g, w_up, conv_w, conv_b, w_down, final_gain, *, seq, tm=512, tf=512):
    m, d = x.shape
    d_ff = w_down.shape[0]
    nj = d_ff // tf
    tiles_per_seq = seq // tm
    final_norm = final_gain is not None

    def per_seq(i, j):
        return (i // tiles_per_seq, 0, 0)

    in_specs = [
        pl.BlockSpec((tm, d), lambda i, j: (i, 0)),
        pl.BlockSpec((1, d), lambda i, j: (0, 0)),
        pl.BlockSpec((1, 1, d), per_seq),
        pl.BlockSpec((1, 1, d), per_seq),
        pl.BlockSpec((1, 1, d), per_seq),
        pl.BlockSpec((d, tf), lambda i, j: (0, j)),
        pl.BlockSpec((d, tf), lambda i, j: (0, nj + j)),
        pl.BlockSpec((CONV_WIDTH, tf), lambda i, j: (0, j)),
        pl.BlockSpec((CONV_WIDTH, tf), lambda i, j: (0, nj + j)),
        pl.BlockSpec((1, tf), lambda i, j: (0, j)),
        pl.BlockSpec((1, tf), lambda i, j: (0, nj + j)),
        pl.BlockSpec((tf, d), lambda i, j: (j, 0)),
    ]
    args = [x, gain, sc, sh, g, w_up, w_up, conv_w, conv_w, conv_b, conv_b, w_down]
    if final_norm:
        in_specs.append(pl.BlockSpec((1, d), lambda i, j: (0, 0)))
        args.append(final_gain)
    return pl.pallas_call(
        functools.partial(_ffn_kernel, tiles_per_seq=tiles_per_seq, final_norm=final_norm),
        grid=(m // tm, nj),
        in_specs=in_specs,
        out_specs=pl.BlockSpec((tm, d), lambda i, j: (i, 0)),
        out_shape=jax.ShapeDtypeStruct((m, d), F32),
        scratch_shapes=[
            pltpu.VMEM((tm, d), BF16),
            pltpu.VMEM((tm, d), F32),
            pltpu.VMEM((nj, 2, SUBLANES, tf), F32),
        ],
        compiler_params=_params("arbitrary", "arbitrary"),
        name="conv_ffn",
    )(*args)


def _pad_cols(w, width):
    return jnp.pad(w, ((0, 0), (0, width - w.shape[1])))


def kernel(x, c, w_mod, b_mod, norm_mix, norm_ffn, gla_w_in, gla_w_gate, gla_b_gate, gla_norm, gla_w_out,
           fox_w_in, fox_b_f, fox_q_norm, fox_k_norm, fox_w_out, ffn_w_up, ffn_conv_w, ffn_conv_b,
           ffn_w_down, norm_final):
    bsz, seq, d = x.shape
    depth = w_mod.shape[0]
    m = bsz * seq
    xf = x.reshape(m, d)

    mod = _modulation(c, w_mod, b_mod).reshape(depth, bsz, 6, 1, d)

    for i in range(depth):
        sh_m, sc_m, g_m, sh_f, sc_f, g_f = (mod[i, :, t] for t in range(6))
        j = i // 2
        if i % 2 == 0:
            n_main = gla_w_in.shape[2] - GLA_RANK
            w_in = gla_w_in[j]
            proj, a = _norm_proj(xf, norm_mix[i][None], sc_m, sh_m, w_in[:, :n_main].astype(BF16),
                                 _pad_cols(w_in[:, n_main:], LANES).astype(BF16), seq=seq)
            wg = jnp.pad(gla_w_gate[j], ((0, LANES - GLA_RANK), (0, 0))).astype(BF16)
            o = _gla_core(proj, a, wg, gla_b_gate[j][None], gla_norm[j][None], bsz=bsz, seq=seq)
            w_out = gla_w_out[j]
        else:
            n_main = fox_w_in.shape[2] - FOX_HEADS
            w_in = fox_w_in[j]
            proj, fl = _norm_proj(xf, norm_mix[i][None], sc_m, sh_m, w_in[:, :n_main].astype(BF16),
                                  _pad_cols(w_in[:, n_main:], LANES).astype(BF16), seq=seq)
            ck = _fox_cum(fl, _pad_cols(fox_b_f[j][None], LANES), bsz=bsz, seq=seq)
            o = _fox_attention(proj, ck.reshape(bsz, FOX_HEADS, 1, seq), fox_q_norm[j][None],
                               fox_k_norm[j][None], bsz=bsz, seq=seq)
            w_out = fox_w_out[j]
        xf = _proj_residual(o, w_out.astype(BF16), xf, g_m, seq=seq)
        xf = _conv_ffn(xf, norm_ffn[i][None], sc_f, sh_f, g_f, ffn_w_up[i].astype(BF16), ffn_conv_w[i],
                       ffn_conv_b[i][None], ffn_w_down[i].astype(BF16),
                       norm_final[None] if i == depth - 1 else None, seq=seq)
    return xf.reshape(bsz, seq, d)
```

```python
import functools

import jax
import jax.numpy as jnp
from jax import lax
from jax.experimental import pallas as pl
from jax.experimental.pallas import tpu as pltpu

F32 = jnp.float32
BF16 = jnp.bfloat16

NORM_EPS = 1e-6
GLA_HEADS = 4
GLA_RANK = 16
GLA_TAU = 16.0
GLA_CHUNK = 64
FOX_HEADS = 16
CONV_WIDTH = 3
FFN_ROW_CHUNK = 32

LANES = 128
SUBLANES = 8
MXU_COLS = 256
V7X_VMEM_BYTES = 64 * 1024 * 1024
VMEM_LIMIT_BYTES = V7X_VMEM_BYTES - 8 * 1024 * 1024

LOG2_E = 1.4426950408889634

NT_DIMS = (((1,), (1,)), ((), ()))
TN_DIMS = (((0,), (0,)), ((), ()))


def _params(*semantics):
    return pltpu.CompilerParams(dimension_semantics=semantics, vmem_limit_bytes=VMEM_LIMIT_BYTES)


def _dot(a, b):
    return jnp.dot(a, b, preferred_element_type=F32)


def _rms_scale(x):
    return x * lax.rsqrt(jnp.mean(x * x, axis=-1, keepdims=True) + NORM_EPS)


def _log_sigmoid(x):
    return jnp.minimum(x, 0.0) - jnp.log1p(jnp.exp(-jnp.abs(x)))


def _split_bf16(x, parts):
    out = []
    for _ in range(parts):
        hi = x.astype(BF16)
        out.append(hi)
        x = x - hi.astype(F32)
    return out


def _mod_kernel(c_ref, w_ref, b_ref, o_ref):
    c = c_ref[...]
    cond = c * jax.nn.sigmoid(c)
    o_ref[0] = _dot(cond.astype(BF16), w_ref[0].astype(BF16)) + b_ref[0]


def _modulation(c, w_mod, b_mod, *, tn=1024):
    depth, d, n = w_mod.shape
    bsz = c.shape[0]
    return pl.pallas_call(
        _mod_kernel,
        grid=(depth, n // tn),
        in_specs=[
            pl.BlockSpec((bsz, d), lambda l, j: (0, 0)),
            pl.BlockSpec((1, d, tn), lambda l, j: (l, 0, j)),
            pl.BlockSpec((1, 1, tn), lambda l, j: (l, 0, j)),
        ],
        out_specs=pl.BlockSpec((1, bsz, tn), lambda l, j: (l, 0, j)),
        out_shape=jax.ShapeDtypeStruct((depth, bsz, n), F32),
        compiler_params=_params("parallel", "parallel"),
        name="adaln_modulation",
    )(c, w_mod, b_mod.reshape(depth, 1, n))


def _norm_proj_kernel(x_ref, gain_ref, sc_ref, sh_ref, w_ref, ws_ref, o_ref, os_ref, h_ref):
    @pl.when(pl.program_id(1) == 0)
    def _():
        h = _rms_scale(x_ref[...]) * gain_ref[...] * (1.0 + sc_ref[0]) + sh_ref[0]
        h_ref[...] = h.astype(BF16)
        os_ref[...] = _dot(h_ref[...], ws_ref[...])

    o_ref[...] = _dot(h_ref[...], w_ref[...]).astype(o_ref.dtype)


def _norm_proj(x, gain, sc, sh, w, w_small, *, seq, tm=1024, tn=1024):
    m, d = x.shape
    n = w.shape[1]
    tiles_per_seq = seq // tm
    return pl.pallas_call(
        _norm_proj_kernel,
        grid=(m // tm, n // tn),
        in_specs=[
            pl.BlockSpec((tm, d), lambda i, j: (i, 0)),
            pl.BlockSpec((1, d), lambda i, j: (0, 0)),
            pl.BlockSpec((1, 1, d), lambda i, j: (i // tiles_per_seq, 0, 0)),
            pl.BlockSpec((1, 1, d), lambda i, j: (i // tiles_per_seq, 0, 0)),
            pl.BlockSpec((d, tn), lambda i, j: (0, j)),
            pl.BlockSpec((d, LANES), lambda i, j: (0, 0)),
        ],
        out_specs=[
            pl.BlockSpec((tm, tn), lambda i, j: (i, j)),
            pl.BlockSpec((tm, LANES), lambda i, j: (i, 0)),
        ],
        out_shape=[
            jax.ShapeDtypeStruct((m, n), BF16),
            jax.ShapeDtypeStruct((m, LANES), F32),
        ],
        scratch_shapes=[pltpu.VMEM((tm, d), BF16)],
        compiler_params=_params("parallel", "arbitrary"),
        name="norm_modulate_in_proj",
    )(x, gain, sc, sh, w, w_small)


def _proj_residual_kernel(a_ref, w_ref, x_ref, g_ref, o_ref):
    o_ref[...] = x_ref[...] + (1.0 + g_ref[0]) * _dot(a_ref[...], w_ref[...])


def _proj_residual(a, w, x, g, *, seq, tm=1024, tn=1024):
    m, k = a.shape
    n = w.shape[1]
    tiles_per_seq = seq // tm
    return pl.pallas_call(
        _proj_residual_kernel,
        grid=(m // tm, n // tn),
        in_specs=[
            pl.BlockSpec((tm, k), lambda i, j: (i, 0)),
            pl.BlockSpec((k, tn), lambda i, j: (0, j)),
            pl.BlockSpec((tm, tn), lambda i, j: (i, j)),
            pl.BlockSpec((1, 1, tn), lambda i, j: (i // tiles_per_seq, 0, j)),
        ],
        out_specs=pl.BlockSpec((tm, tn), lambda i, j: (i, j)),
        out_shape=jax.ShapeDtypeStruct((m, n), F32),
        compiler_params=_params("parallel", "parallel"),
        name="out_proj_residual",
    )(a, w, x, g)


def _gla_kernel(q_ref, k_ref, v_ref, r_ref, a_ref, wg_ref, bg_ref, gn_ref, o_ref, state_ref, *, chunk):
    @pl.when(pl.program_id(2) == 0)
    def _():
        state_ref[...] = jnp.zeros_like(state_ref)

    rows, dk = q_ref.shape
    q_scale = dk ** -0.5
    log_alpha = _log_sigmoid(_dot(a_ref[...].astype(BF16), wg_ref[...]) + bg_ref[...]) / GLA_TAU

    row = lax.broadcasted_iota(jnp.int32, (chunk, chunk), 0)
    col = lax.broadcasted_iota(jnp.int32, (chunk, chunk), 1)
    causal = row >= col
    tri = causal.astype(BF16)

    for c in range(rows // chunk):
        sl = pl.ds(c * chunk, chunk)
        la = log_alpha[c * chunk:(c + 1) * chunk]
        b = sum(_dot(tri, part) for part in _split_bf16(la, 2))
        b_last = b[chunk - 1:chunk]
        q = q_ref[sl, :].astype(F32)
        k = k_ref[sl, :].astype(F32)
        v = v_ref[sl, :]
        q_dec = (q * (q_scale * jnp.exp(b))).astype(BF16)
        k_inv = (k * jnp.exp(-b)).astype(BF16)
        k_end = (k * jnp.exp(b_last - b)).astype(BF16)

        attn = jnp.where(causal, lax.dot_general(q_dec, k_inv, NT_DIMS, preferred_element_type=F32), 0.0)
        state = state_ref[...]
        o = _dot(attn.astype(BF16), v)
        o = o + lax.dot_general(q_dec, state.astype(BF16), NT_DIMS, preferred_element_type=F32)
        state_ref[...] = state * jnp.exp(b_last) + lax.dot_general(v, k_end, TN_DIMS, preferred_element_type=F32)

        r = r_ref[sl, :].astype(F32)
        o = (_rms_scale(o) * gn_ref[...]) * (r * jax.nn.sigmoid(r))
        o_ref[sl, :] = o.astype(o_ref.dtype)


def _gla_core(proj, a, wg, bg, gn, *, bsz, seq, rows=512):
    m = proj.shape[0]
    dk = wg.shape[1] // GLA_HEADS
    dv = gn.shape[1] // GLA_HEADS
    nblk = seq // rows
    k_off = (GLA_HEADS * dk) // dk
    v_off = (2 * GLA_HEADS * dk) // dv
    r_off = v_off + GLA_HEADS

    def row_blk(b, h, t):
        return b * nblk + t

    return pl.pallas_call(
        functools.partial(_gla_kernel, chunk=GLA_CHUNK),
        grid=(bsz, GLA_HEADS, nblk),
        in_specs=[
            pl.BlockSpec((rows, dk), lambda b, h, t: (row_blk(b, h, t), h)),
            pl.BlockSpec((rows, dk), lambda b, h, t: (row_blk(b, h, t), k_off + h)),
            pl.BlockSpec((rows, dv), lambda b, h, t: (row_blk(b, h, t), v_off + h)),
            pl.BlockSpec((rows, dv), lambda b, h, t: (row_blk(b, h, t), r_off + h)),
            pl.BlockSpec((rows, LANES), lambda b, h, t: (row_blk(b, h, t), 0)),
            pl.BlockSpec((LANES, dk), lambda b, h, t: (0, h)),
            pl.BlockSpec((1, dk), lambda b, h, t: (0, h)),
            pl.BlockSpec((1, dv), lambda b, h, t: (0, h)),
        ],
        out_specs=pl.BlockSpec((rows, dv), lambda b, h, t: (row_blk(b, h, t), h)),
        out_shape=jax.ShapeDtypeStruct((m, GLA_HEADS * dv), BF16),
        scratch_shapes=[pltpu.VMEM((dv, dk), F32)],
        compiler_params=_params("parallel", "parallel", "arbitrary"),
        name="gla_chunked",
    )(proj, proj, proj, proj, a, wg, bg, gn)


def _fox_cum_kernel(fl_ref, bf_ref, o_ref, *, blk):
    seq = fl_ref.shape[0]
    log_f = _log_sigmoid(fl_ref[...] + bf_ref[...])
    row = lax.broadcasted_iota(jnp.int32, (blk, blk), 0)
    col = lax.broadcasted_iota(jnp.int32, (blk, blk), 1)
    tri = (row >= col).astype(BF16)
    carry = jnp.zeros((1, LANES), F32)
    pieces = []
    for s in range(seq // blk):
        lf = log_f[s * blk:(s + 1) * blk]
        cum = sum(_dot(tri, part) for part in _split_bf16(lf, 3)) + carry
        carry = cum[blk - 1:blk]
        pieces.append(cum)
    cum = jnp.concatenate(pieces, axis=0)
    o_ref[0] = cum.T[:o_ref.shape[1]]


def _fox_cum(fl, bf, *, bsz, seq, blk=256):
    return pl.pallas_call(
        functools.partial(_fox_cum_kernel, blk=blk),
        grid=(bsz,),
        in_specs=[
            pl.BlockSpec((seq, LANES), lambda b: (b, 0)),
            pl.BlockSpec((1, LANES), lambda b: (0, 0)),
        ],
        out_specs=pl.BlockSpec((1, FOX_HEADS, seq), lambda b: (b, 0, 0)),
        out_shape=jax.ShapeDtypeStruct((bsz, FOX_HEADS, seq), F32),
        compiler_params=_params("parallel"),
        name="fox_cum_log_forget",
    )(fl, bf)


def _fox_attn_kernel(q_ref, k_ref, v_ref, og_ref, ck_ref, qg_ref, kg_ref, o_ref, kn_ref, qn_ref, m_ref, l_ref,
                     acc_ref, *, heads, hd, rs, tk):
    qi = pl.program_id(2)
    tq = q_ref.shape[0]

    @pl.when(qi == 0)
    def _():
        for g in range(heads):
            cols = slice(g * hd, (g + 1) * hd)
            kn_ref[:, cols] = (_rms_scale(k_ref[:, cols].astype(F32)) * kg_ref[...]).astype(BF16)

    m_ref[...] = jnp.full(m_ref.shape, -jnp.inf, F32)
    l_ref[...] = jnp.zeros(l_ref.shape, F32)
    acc_ref[...] = jnp.zeros(acc_ref.shape, F32)

    q_scale = LOG2_E * hd ** -0.5
    for g in range(heads):
        cols = slice(g * hd, (g + 1) * hd)
        qn_ref[:, cols] = ((_rms_scale(q_ref[:, cols].astype(F32)) * qg_ref[...]) * q_scale).astype(BF16)

    def scores(unit):
        g, r, off, _ = unit
        cols = slice(g * hd, (g + 1) * hd)
        kb = kn_ref[pl.ds(off, tk), cols]
        ckb = ck_ref[0, g, :, pl.ds(off, tk)] * LOG2_E
        return lax.dot_general(qn_ref[r * rs:(r + 1) * rs, cols], kb, NT_DIMS, preferred_element_type=F32) - ckb

    def softmax_update(unit, s):
        g, r, _, diag_shift = unit
        rows = slice(r * rs, (r + 1) * rs)
        if diag_shift is not None:
            visible = (lax.broadcasted_iota(jnp.int32, (rs, tk), 1) + diag_shift
                       <= lax.broadcasted_iota(jnp.int32, (rs, tk), 0))
            s = jnp.where(visible, s, -jnp.inf)
        m_prev = m_ref[g, rows]
        m_new = jnp.maximum(m_prev, jnp.max(s, axis=-1, keepdims=True))
        alpha = jnp.exp2(m_prev - m_new)
        p = jnp.exp2(s - jnp.concatenate([m_new] * (tk // LANES), axis=1))
        l_ref[g, rows] = alpha * l_ref[g, rows] + jnp.sum(p, axis=-1, keepdims=True)
        m_ref[g, rows] = m_new
        return p.astype(BF16), alpha

    def weighted_values(unit, p, alpha):
        g, r, off, _ = unit
        rows = slice(r * rs, (r + 1) * rs)
        vb = v_ref[pl.ds(off, tk), g * hd:(g + 1) * hd]
        acc_ref[g, rows] = alpha * acc_ref[g, rows] + _dot(p, vb)

    def run(units, lookahead=1):
        s = {u: scores(units[u]) for u in range(min(lookahead, len(units)))}
        for u, unit in enumerate(units):
            p, alpha = softmax_update(unit, s.pop(u))
            if u + lookahead < len(units):
                s[u + lookahead] = scores(units[u + lookahead])
            weighted_values(unit, p, alpha)

    def body(j, carry):
        off = pl.multiple_of(j * tk, tk)
        run([(g, r, off, None) for g in range(heads) for r in range(tq // rs)])
        return carry

    lax.fori_loop(0, qi * (tq // tk), body, 0)

    diag_units = []
    for c in range(tq // tk):
        for g in range(heads):
            for r in range(tq // rs):
                first_key, first_row = c * tk, r * rs
                if first_key > first_row + rs - 1:
                    continue
                unmasked = first_key + tk - 1 <= first_row
                diag_units.append((g, r, pl.multiple_of(qi * tq + first_key, tk),
                                   None if unmasked else first_key - first_row))
    run(diag_units)

    for g in range(heads):
        cols = slice(g * hd, (g + 1) * hd)
        og = og_ref[:, cols].astype(F32)
        o_ref[:, cols] = ((acc_ref[g] / l_ref[g]) * jax.nn.sigmoid(og)).astype(o_ref.dtype)


def _fox_attention(proj, ck, qg, kg, *, bsz, seq, tq=512, heads=4, rs=128, tk=256):
    m = proj.shape[0]
    hd = qg.shape[1]
    assert hd == LANES, "softmax statistics are kept lane-replicated at the head width"
    nq = seq // tq
    groups = FOX_HEADS // heads
    w = heads * hd
    return pl.pallas_call(
        functools.partial(_fox_attn_kernel, heads=heads, hd=hd, rs=rs, tk=tk),
        grid=(bsz, groups, nq),
        in_specs=[
            pl.BlockSpec((tq, w), lambda b, h, i: (b * nq + i, h)),
            pl.BlockSpec((seq, w), lambda b, h, i: (b, groups + h)),
            pl.BlockSpec((seq, w), lambda b, h, i: (b, 2 * groups + h)),
            pl.BlockSpec((tq, w), lambda b, h, i: (b * nq + i, 3 * groups + h)),
            pl.BlockSpec((1, heads, 1, seq), lambda b, h, i: (b, h, 0, 0)),
            pl.BlockSpec((1, hd), lambda b, h, i: (0, 0)),
            pl.BlockSpec((1, hd), lambda b, h, i: (0, 0)),
        ],
        out_specs=pl.BlockSpec((tq, w), lambda b, h, i: (b * nq + i, h)),
        out_shape=jax.ShapeDtypeStruct((m, FOX_HEADS * hd), BF16),
        scratch_shapes=[
            pltpu.VMEM((seq, w), BF16),
            pltpu.VMEM((tq, w), BF16),
            pltpu.VMEM((heads, tq, LANES), F32),
            pltpu.VMEM((heads, tq, LANES), F32),
            pltpu.VMEM((heads, tq, hd), F32),
        ],
        compiler_params=_params("parallel", "parallel", "arbitrary"),
        name="fox_attention",
    )(proj, proj, proj, proj, ck, qg, kg)


def _ffn_kernel(x_ref, gain_ref, sc_ref, sh_ref, g_ref, wg_ref, wv_ref, cwg_ref, cwv_ref, cbg_ref, cbv_ref,
                wd_ref, *rest, tiles_per_seq, final_norm, nj):
    if final_norm:
        fin_ref, o_ref, h_ref, acc_ref, carry_ref, act_ref, u_ref = rest
    else:
        o_ref, h_ref, acc_ref, carry_ref, act_ref, u_ref = rest
    i = pl.program_id(0)
    j = pl.program_id(1)
    tm = x_ref.shape[0]

    def stage_up(slot):
        h = h_ref[...]
        u_ref[slot, 0, SUBLANES:] = _dot(h, wg_ref[...])
        u_ref[slot, 1, SUBLANES:] = _dot(h, wv_ref[...])

    def stage_act(slot):
        for k in range(2):
            u_ref[slot, k, :SUBLANES] = carry_ref[j - 1, k]
            carry_ref[j - 1, k] = u_ref[slot, k, tm:]

        def conv(k, r0, cw_ref, cb_ref):
            out = cb_ref[...]
            for s in range(CONV_WIDTH):
                rows = slice(SUBLANES - s + r0, SUBLANES - s + r0 + FFN_ROW_CHUNK)
                out = out + u_ref[slot, k, rows] * cw_ref[CONV_WIDTH - 1 - s:CONV_WIDTH - s]
            return out

        for r0 in range(0, tm, FFN_ROW_CHUNK):
            gate = conv(0, r0, cwg_ref, cbg_ref)
            val = conv(1, r0, cwv_ref, cbv_ref)
            act_ref[slot, r0:r0 + FFN_ROW_CHUNK] = ((gate * jax.nn.sigmoid(gate)) * val).astype(BF16)

    def stage_down(slot):
        acc_ref[...] += _dot(act_ref[slot], wd_ref[...])

    @pl.when(j == 0)
    def _():
        h = _rms_scale(x_ref[...]) * gain_ref[...] * (1.0 + sc_ref[0]) + sh_ref[0]
        h_ref[...] = h.astype(BF16)
        acc_ref[...] = jnp.zeros_like(acc_ref)

        @pl.when(i % tiles_per_seq == 0)
        def _():
            carry_ref[...] = jnp.zeros_like(carry_ref)

        stage_up(0)

    @pl.when(j == 1)
    def _():
        stage_act(0)
        stage_up(1)

    for parity in range(2):
        @pl.when((j > 1) & (j < nj) & (j % 2 == parity))
        def _():
            stage_act(1 - parity)
            stage_up(parity)
            stage_down(parity)

    @pl.when(j == nj)
    def _():
        stage_act((nj - 1) % 2)
        stage_down(nj % 2)

    @pl.when(j == nj + 1)
    def _():
        stage_down((nj + 1) % 2)
        y = x_ref[...] + (1.0 + g_ref[0]) * acc_ref[...]
        if final_norm:
            y = _rms_scale(y) * fin_ref[...]
        o_ref[...] = y


def _conv_ffn(x, gain, sc, sh, g, w_up, conv_w, conv_b, w_down, final_gain, *, seq, tm=512, tf=512):
    m, d = x.shape
    d_ff = w_down.shape[0]
    nj = d_ff // tf
    tiles_per_seq = seq // tm
    final_norm = final_gain is not None

    def per_seq(i, j):
        return (i // tiles_per_seq, 0, 0)

    def blk(j, lag):
        return jnp.clip(j - lag, 0, nj - 1)

    in_specs = [
        pl.BlockSpec((tm, d), lambda i, j: (i, 0)),
        pl.BlockSpec((1, d), lambda i, j: (0, 0)),
        pl.BlockSpec((1, 1, d), per_seq),
        pl.BlockSpec((1, 1, d), per_seq),
        pl.BlockSpec((1, 1, d), per_seq),
        pl.BlockSpec((d, tf), lambda i, j: (0, blk(j, 0))),
        pl.BlockSpec((d, tf), lambda i, j: (0, nj + blk(j, 0))),
        pl.BlockSpec((CONV_WIDTH, tf), lambda i, j: (0, blk(j, 1))),
        pl.BlockSpec((CONV_WIDTH, tf), lambda i, j: (0, nj + blk(j, 1))),
        pl.BlockSpec((1, tf), lambda i, j: (0, blk(j, 1))),
        pl.BlockSpec((1, tf), lambda i, j: (0, nj + blk(j, 1))),
        pl.BlockSpec((tf, d), lambda i, j: (blk(j, 2), 0)),
    ]
    args = [x, gain, sc, sh, g, w_up, w_up, conv_w, conv_w, conv_b, conv_b, w_down]
    if final_norm:
        in_specs.append(pl.BlockSpec((1, d), lambda i, j: (0, 0)))
        args.append(final_gain)
    return pl.pallas_call(
        functools.partial(_ffn_kernel, tiles_per_seq=tiles_per_seq, final_norm=final_norm, nj=nj),
        grid=(m // tm, nj + 2),
        in_specs=in_specs,
        out_specs=pl.BlockSpec((tm, d), lambda i, j: (i, 0)),
        out_shape=jax.ShapeDtypeStruct((m, d), F32),
        scratch_shapes=[
            pltpu.VMEM((tm, d), BF16),
            pltpu.VMEM((tm, d), F32),
            pltpu.VMEM((nj, 2, SUBLANES, tf), F32),
            pltpu.VMEM((2, tm, tf), BF16),
            pltpu.VMEM((2, 2, SUBLANES + tm, tf), F32),
        ],
        compiler_params=_params("arbitrary", "arbitrary"),
        name="conv_ffn",
    )(*args)


def _pad_cols(w, width):
    return jnp.pad(w, ((0, 0), (0, width - w.shape[1])))


def kernel(x, c, w_mod, b_mod, norm_mix, norm_ffn, gla_w_in, gla_w_gate, gla_b_gate, gla_norm, gla_w_out,
           fox_w_in, fox_b_f, fox_q_norm, fox_k_norm, fox_w_out, ffn_w_up, ffn_conv_w, ffn_conv_b,
           ffn_w_down, norm_final):
    bsz, seq, d = x.shape
    depth = w_mod.shape[0]
    m = bsz * seq
    xf = x.reshape(m, d)

    mod = _modulation(c, w_mod, b_mod).reshape(depth, bsz, 6, 1, d)

    for i in range(depth):
        sh_m, sc_m, g_m, sh_f, sc_f, g_f = (mod[i, :, t] for t in range(6))
        j = i // 2
        if i % 2 == 0:
            n_main = gla_w_in.shape[2] - GLA_RANK
            w_in = gla_w_in[j]
            proj, a = _norm_proj(xf, norm_mix[i][None], sc_m, sh_m, w_in[:, :n_main].astype(BF16),
                                 _pad_cols(w_in[:, n_main:], LANES).astype(BF16), seq=seq)
            wg = jnp.pad(gla_w_gate[j], ((0, LANES - GLA_RANK), (0, 0))).astype(BF16)
            o = _gla_core(proj, a, wg, gla_b_gate[j][None], gla_norm[j][None], bsz=bsz, seq=seq)
            w_out = gla_w_out[j]
        else:
            n_main = fox_w_in.shape[2] - FOX_HEADS
            w_in = fox_w_in[j]
            proj, fl = _norm_proj(xf, norm_mix[i][None], sc_m, sh_m, w_in[:, :n_main].astype(BF16),
                                  _pad_cols(w_in[:, n_main:], LANES).astype(BF16), seq=seq)
            ck = _fox_cum(fl, _pad_cols(fox_b_f[j][None], LANES), bsz=bsz, seq=seq)
            o = _fox_attention(proj, ck.reshape(bsz, FOX_HEADS, 1, seq), fox_q_norm[j][None],
                               fox_k_norm[j][None], bsz=bsz, seq=seq)
            w_out = fox_w_out[j]
        xf = _proj_residual(o, w_out.astype(BF16), xf, g_m, seq=seq)
        xf = _conv_ffn(xf, norm_ffn[i][None], sc_f, sh_f, g_f, ffn_w_up[i].astype(BF16), ffn_conv_w[i],
                       ffn_conv_b[i][None], ffn_w_down[i].astype(BF16),
                       norm_final[None] if i == depth - 1 else None, seq=seq)
    return xf.reshape(bsz, seq, d)
```

```python
import functools

import jax
import jax.numpy as jnp
from jax import lax
from jax.experimental import pallas as pl
from jax.experimental.pallas import tpu as pltpu

F32 = jnp.float32
BF16 = jnp.bfloat16

NORM_EPS = 1e-6
GLA_HEADS = 4
GLA_RANK = 16
GLA_TAU = 16.0
GLA_CHUNK = 64
FOX_HEADS = 16
CONV_WIDTH = 3
FFN_ROW_CHUNK = 32

LANES = 128
SUBLANES = 8
MXU_COLS = 256
V7X_VMEM_BYTES = 64 * 1024 * 1024
VMEM_LIMIT_BYTES = V7X_VMEM_BYTES - 8 * 1024 * 1024

LOG2_E = 1.4426950408889634

NT_DIMS = (((1,), (1,)), ((), ()))
TN_DIMS = (((0,), (0,)), ((), ()))


def _params(*semantics):
    return pltpu.CompilerParams(dimension_semantics=semantics, vmem_limit_bytes=VMEM_LIMIT_BYTES)


def _dot(a, b):
    return jnp.dot(a, b, preferred_element_type=F32)


def _rms_scale(x):
    return x * lax.rsqrt(jnp.mean(x * x, axis=-1, keepdims=True) + NORM_EPS)


def _log_sigmoid(x):
    return jnp.minimum(x, 0.0) - jnp.log1p(jnp.exp(-jnp.abs(x)))


def _split_bf16(x, parts):
    out = []
    for _ in range(parts):
        hi = x.astype(BF16)
        out.append(hi)
        x = x - hi.astype(F32)
    return out


def _mod_kernel(c_ref, w_ref, b_ref, o_ref):
    c = c_ref[...]
    cond = c * jax.nn.sigmoid(c)
    o_ref[0] = _dot(cond.astype(BF16), w_ref[0].astype(BF16)) + b_ref[0]


def _modulation(c, w_mod, b_mod, *, tn=1024):
    depth, d, n = w_mod.shape
    bsz = c.shape[0]
    return pl.pallas_call(
        _mod_kernel,
        grid=(depth, n // tn),
        in_specs=[
            pl.BlockSpec((bsz, d), lambda l, j: (0, 0)),
            pl.BlockSpec((1, d, tn), lambda l, j: (l, 0, j)),
            pl.BlockSpec((1, 1, tn), lambda l, j: (l, 0, j)),
        ],
        out_specs=pl.BlockSpec((1, bsz, tn), lambda l, j: (l, 0, j)),
        out_shape=jax.ShapeDtypeStruct((depth, bsz, n), F32),
        compiler_params=_params("parallel", "parallel"),
        name="adaln_modulation",
    )(c, w_mod, b_mod.reshape(depth, 1, n))


def _norm_proj_kernel(x_ref, gain_ref, sc_ref, sh_ref, w_ref, ws_ref, o_ref, os_ref, h_ref):
    @pl.when(pl.program_id(1) == 0)
    def _():
        h = _rms_scale(x_ref[...]) * gain_ref[...] * (1.0 + sc_ref[0]) + sh_ref[0]
        h_ref[...] = h.astype(BF16)
        os_ref[...] = _dot(h_ref[...], ws_ref[...])

    o_ref[...] = _dot(h_ref[...], w_ref[...]).astype(o_ref.dtype)


def _norm_proj(x, gain, sc, sh, w, w_small, *, seq, tm=1024, tn=1024):
    m, d = x.shape
    n = w.shape[1]
    tiles_per_seq = seq // tm
    return pl.pallas_call(
        _norm_proj_kernel,
        grid=(m // tm, n // tn),
        in_specs=[
            pl.BlockSpec((tm, d), lambda i, j: (i, 0)),
            pl.BlockSpec((1, d), lambda i, j: (0, 0)),
            pl.BlockSpec((1, 1, d), lambda i, j: (i // tiles_per_seq, 0, 0)),
            pl.BlockSpec((1, 1, d), lambda i, j: (i // tiles_per_seq, 0, 0)),
            pl.BlockSpec((d, tn), lambda i, j: (0, j)),
            pl.BlockSpec((d, LANES), lambda i, j: (0, 0)),
        ],
        out_specs=[
            pl.BlockSpec((tm, tn), lambda i, j: (i, j)),
            pl.BlockSpec((tm, LANES), lambda i, j: (i, 0)),
        ],
        out_shape=[
            jax.ShapeDtypeStruct((m, n), BF16),
            jax.ShapeDtypeStruct((m, LANES), F32),
        ],
        scratch_shapes=[pltpu.VMEM((tm, d), BF16)],
        compiler_params=_params("parallel", "arbitrary"),
        name="norm_modulate_in_proj",
    )(x, gain, sc, sh, w, w_small)


def _proj_residual_kernel(a_ref, w_ref, x_ref, g_ref, o_ref):
    o_ref[...] = x_ref[...] + (1.0 + g_ref[0]) * _dot(a_ref[...], w_ref[...])


def _proj_residual(a, w, x, g, *, seq, tm=1024, tn=1024):
    m, k = a.shape
    n = w.shape[1]
    tiles_per_seq = seq // tm
    return pl.pallas_call(
        _proj_residual_kernel,
        grid=(m // tm, n // tn),
        in_specs=[
            pl.BlockSpec((tm, k), lambda i, j: (i, 0)),
            pl.BlockSpec((k, tn), lambda i, j: (0, j)),
            pl.BlockSpec((tm, tn), lambda i, j: (i, j)),
            pl.BlockSpec((1, 1, tn), lambda i, j: (i // tiles_per_seq, 0, j)),
        ],
        out_specs=pl.BlockSpec((tm, tn), lambda i, j: (i, j)),
        out_shape=jax.ShapeDtypeStruct((m, n), F32),
        compiler_params=_params("parallel", "parallel"),
        name="out_proj_residual",
    )(a, w, x, g)


def _gla_kernel(q_ref, k_ref, v_ref, r_ref, a_ref, wg_ref, bg_ref, gn_ref, o_ref, state_ref, *, chunk):
    @pl.when(pl.program_id(2) == 0)
    def _():
        state_ref[...] = jnp.zeros_like(state_ref)

    rows, dk = q_ref.shape
    q_scale = dk ** -0.5
    log_alpha = _log_sigmoid(_dot(a_ref[...].astype(BF16), wg_ref[...]) + bg_ref[...]) / GLA_TAU

    row = lax.broadcasted_iota(jnp.int32, (chunk, chunk), 0)
    col = lax.broadcasted_iota(jnp.int32, (chunk, chunk), 1)
    causal = row >= col
    tri = causal.astype(BF16)

    for c in range(rows // chunk):
        sl = pl.ds(c * chunk, chunk)
        la = log_alpha[c * chunk:(c + 1) * chunk]
        b = sum(_dot(tri, part) for part in _split_bf16(la, 2))
        b_last = b[chunk - 1:chunk]
        q = q_ref[sl, :].astype(F32)
        k = k_ref[sl, :].astype(F32)
        v = v_ref[sl, :]
        q_dec = (q * (q_scale * jnp.exp(b))).astype(BF16)
        k_inv = (k * jnp.exp(-b)).astype(BF16)
        k_end = (k * jnp.exp(b_last - b)).astype(BF16)

        attn = jnp.where(causal, lax.dot_general(q_dec, k_inv, NT_DIMS, preferred_element_type=F32), 0.0)
        state = state_ref[...]
        o = _dot(attn.astype(BF16), v)
        o = o + lax.dot_general(q_dec, state.astype(BF16), NT_DIMS, preferred_element_type=F32)
        state_ref[...] = state * jnp.exp(b_last) + lax.dot_general(v, k_end, TN_DIMS, preferred_element_type=F32)

        r = r_ref[sl, :].astype(F32)
        o = (_rms_scale(o) * gn_ref[...]) * (r * jax.nn.sigmoid(r))
        o_ref[sl, :] = o.astype(o_ref.dtype)


def _gla_core(proj, a, wg, bg, gn, *, bsz, seq, rows=512):
    m = proj.shape[0]
    dk = wg.shape[1] // GLA_HEADS
    dv = gn.shape[1] // GLA_HEADS
    nblk = seq // rows
    k_off = (GLA_HEADS * dk) // dk
    v_off = (2 * GLA_HEADS * dk) // dv
    r_off = v_off + GLA_HEADS

    def row_blk(b, h, t):
        return b * nblk + t

    return pl.pallas_call(
        functools.partial(_gla_kernel, chunk=GLA_CHUNK),
        grid=(bsz, GLA_HEADS, nblk),
        in_specs=[
            pl.BlockSpec((rows, dk), lambda b, h, t: (row_blk(b, h, t), h)),
            pl.BlockSpec((rows, dk), lambda b, h, t: (row_blk(b, h, t), k_off + h)),
            pl.BlockSpec((rows, dv), lambda b, h, t: (row_blk(b, h, t), v_off + h)),
            pl.BlockSpec((rows, dv), lambda b, h, t: (row_blk(b, h, t), r_off + h)),
            pl.BlockSpec((rows, LANES), lambda b, h, t: (row_blk(b, h, t), 0)),
            pl.BlockSpec((LANES, dk), lambda b, h, t: (0, h)),
            pl.BlockSpec((1, dk), lambda b, h, t: (0, h)),
            pl.BlockSpec((1, dv), lambda b, h, t: (0, h)),
        ],
        out_specs=pl.BlockSpec((rows, dv), lambda b, h, t: (row_blk(b, h, t), h)),
        out_shape=jax.ShapeDtypeStruct((m, GLA_HEADS * dv), BF16),
        scratch_shapes=[pltpu.VMEM((dv, dk), F32)],
        compiler_params=_params("parallel", "parallel", "arbitrary"),
        name="gla_chunked",
    )(proj, proj, proj, proj, a, wg, bg, gn)


def _fox_cum_kernel(fl_ref, bf_ref, o_ref, *, blk):
    seq = fl_ref.shape[0]
    log_f = _log_sigmoid(fl_ref[...] + bf_ref[...])
    row = lax.broadcasted_iota(jnp.int32, (blk, blk), 0)
    col = lax.broadcasted_iota(jnp.int32, (blk, blk), 1)
    tri = (row >= col).astype(BF16)
    carry = jnp.zeros((1, LANES), F32)
    pieces = []
    for s in range(seq // blk):
        lf = log_f[s * blk:(s + 1) * blk]
        cum = sum(_dot(tri, part) for part in _split_bf16(lf, 3)) + carry
        carry = cum[blk - 1:blk]
        pieces.append(cum)
    cum = jnp.concatenate(pieces, axis=0)
    o_ref[0] = cum.T[:o_ref.shape[1]]


def _fox_cum(fl, bf, *, bsz, seq, blk=256):
    return pl.pallas_call(
        functools.partial(_fox_cum_kernel, blk=blk),
        grid=(bsz,),
        in_specs=[
            pl.BlockSpec((seq, LANES), lambda b: (b, 0)),
            pl.BlockSpec((1, LANES), lambda b: (0, 0)),
        ],
        out_specs=pl.BlockSpec((1, FOX_HEADS, seq), lambda b: (b, 0, 0)),
        out_shape=jax.ShapeDtypeStruct((bsz, FOX_HEADS, seq), F32),
        compiler_params=_params("parallel"),
        name="fox_cum_log_forget",
    )(fl, bf)


def _fox_attn_kernel(q_ref, k_ref, v_ref, og_ref, ck_ref, qg_ref, kg_ref, o_ref, kn_ref, qn_ref, m_ref, l_ref,
                     acc_ref, *, heads, hd, rs, tk):
    qi = pl.program_id(2)
    tq = q_ref.shape[0]

    @pl.when(qi == 0)
    def _():
        for g in range(heads):
            cols = slice(g * hd, (g + 1) * hd)
            kn_ref[:, cols] = (_rms_scale(k_ref[:, cols].astype(F32)) * kg_ref[...]).astype(BF16)

    m_ref[...] = jnp.full(m_ref.shape, -jnp.inf, F32)
    l_ref[...] = jnp.zeros(l_ref.shape, F32)
    acc_ref[...] = jnp.zeros(acc_ref.shape, F32)

    q_scale = LOG2_E * hd ** -0.5
    for g in range(heads):
        cols = slice(g * hd, (g + 1) * hd)
        qn_ref[:, cols] = ((_rms_scale(q_ref[:, cols].astype(F32)) * qg_ref[...]) * q_scale).astype(BF16)

    def scores(unit):
        g, r, off, _ = unit
        cols = slice(g * hd, (g + 1) * hd)
        kb = kn_ref[pl.ds(off, tk), cols]
        ckb = ck_ref[0, g, :, pl.ds(off, tk)] * LOG2_E
        return lax.dot_general(qn_ref[r * rs:(r + 1) * rs, cols], kb, NT_DIMS, preferred_element_type=F32) - ckb

    def softmax_update(unit, s):
        g, r, _, diag_shift = unit
        rows = slice(r * rs, (r + 1) * rs)
        if diag_shift is not None:
            visible = (lax.broadcasted_iota(jnp.int32, (rs, tk), 1) + diag_shift
                       <= lax.broadcasted_iota(jnp.int32, (rs, tk), 0))
            s = jnp.where(visible, s, -jnp.inf)
        m_prev = m_ref[g, rows]
        m_new = jnp.maximum(m_prev, jnp.max(s, axis=-1, keepdims=True))
        alpha = jnp.exp2(m_prev - m_new)
        p = jnp.exp2(s - jnp.concatenate([m_new] * (tk // LANES), axis=1))
        l_ref[g, rows] = alpha * l_ref[g, rows] + jnp.sum(p, axis=-1, keepdims=True)
        m_ref[g, rows] = m_new
        return p.astype(BF16), alpha

    def weighted_values(unit, p, alpha):
        g, r, off, _ = unit
        rows = slice(r * rs, (r + 1) * rs)
        vb = v_ref[pl.ds(off, tk), g * hd:(g + 1) * hd]
        acc_ref[g, rows] = alpha * acc_ref[g, rows] + _dot(p, vb)

    def run(units, lookahead=1):
        s = {u: scores(units[u]) for u in range(min(lookahead, len(units)))}
        for u, unit in enumerate(units):
            p, alpha = softmax_update(unit, s.pop(u))
            if u + lookahead < len(units):
                s[u + lookahead] = scores(units[u + lookahead])
            weighted_values(unit, p, alpha)

    def body(j, carry):
        off = pl.multiple_of(j * tk, tk)
        run([(g, r, off, None) for g in range(heads) for r in range(tq // rs)])
        return carry

    lax.fori_loop(0, qi * (tq // tk), body, 0)

    diag_units = []
    for c in range(tq // tk):
        for g in range(heads):
            for r in range(tq // rs):
                first_key, first_row = c * tk, r * rs
                if first_key > first_row + rs - 1:
                    continue
                unmasked = first_key + tk - 1 <= first_row
                diag_units.append((g, r, pl.multiple_of(qi * tq + first_key, tk),
                                   None if unmasked else first_key - first_row))
    run(diag_units)

    for g in range(heads):
        cols = slice(g * hd, (g + 1) * hd)
        og = og_ref[:, cols].astype(F32)
        o_ref[:, cols] = ((acc_ref[g] / l_ref[g]) * jax.nn.sigmoid(og)).astype(o_ref.dtype)


def _fox_attention(proj, ck, qg, kg, *, bsz, seq, tq=512, heads=4, rs=128, tk=256):
    m = proj.shape[0]
    hd = qg.shape[1]
    assert hd == LANES, "softmax statistics are kept lane-replicated at the head width"
    nq = seq // tq
    groups = FOX_HEADS // heads
    w = heads * hd
    return pl.pallas_call(
        functools.partial(_fox_attn_kernel, heads=heads, hd=hd, rs=rs, tk=tk),
        grid=(bsz, groups, nq),
        in_specs=[
            pl.BlockSpec((tq, w), lambda b, h, i: (b * nq + i, h)),
            pl.BlockSpec((seq, w), lambda b, h, i: (b, groups + h)),
            pl.BlockSpec((seq, w), lambda b, h, i: (b, 2 * groups + h)),
            pl.BlockSpec((tq, w), lambda b, h, i: (b * nq + i, 3 * groups + h)),
            pl.BlockSpec((1, heads, 1, seq), lambda b, h, i: (b, h, 0, 0)),
            pl.BlockSpec((1, hd), lambda b, h, i: (0, 0)),
            pl.BlockSpec((1, hd), lambda b, h, i: (0, 0)),
        ],
        out_specs=pl.BlockSpec((tq, w), lambda b, h, i: (b * nq + i, h)),
        out_shape=jax.ShapeDtypeStruct((m, FOX_HEADS * hd), BF16),
        scratch_shapes=[
            pltpu.VMEM((seq, w), BF16),
            pltpu.VMEM((tq, w), BF16),
            pltpu.VMEM((heads, tq, LANES), F32),
            pltpu.VMEM((heads, tq, LANES), F32),
            pltpu.VMEM((heads, tq, hd), F32),
        ],
        compiler_params=_params("parallel", "parallel", "arbitrary"),
        name="fox_attention",
    )(proj, proj, proj, proj, ck, qg, kg)


def _ffn_kernel(x_ref, gain_ref, sc_ref, sh_ref, g_ref, wg_ref, wv_ref, cwg_ref, cwv_ref, cbg_ref, cbv_ref,
                wd_ref, *rest, tiles_per_seq, final_norm, nj):
    if final_norm:
        fin_ref, o_ref, h_ref, carry_ref, act_ref, u_ref = rest
    else:
        o_ref, h_ref, carry_ref, act_ref, u_ref = rest
    i = pl.program_id(0)
    j = pl.program_id(1)
    tm = x_ref.shape[0]

    def stage_up(slot):
        h = h_ref[...]
        u_ref[slot, 0, SUBLANES:] = _dot(h, wg_ref[...])
        u_ref[slot, 1, SUBLANES:] = _dot(h, wv_ref[...])

    def stage_act(slot):
        for k in range(2):
            u_ref[slot, k, :SUBLANES] = carry_ref[j - 1, k]
            carry_ref[j - 1, k] = u_ref[slot, k, tm:]

        def conv(k, r0, cw_ref, cb_ref):
            out = cb_ref[...]
            for s in range(CONV_WIDTH):
                rows = slice(SUBLANES - s + r0, SUBLANES - s + r0 + FFN_ROW_CHUNK)
                out = out + u_ref[slot, k, rows] * cw_ref[CONV_WIDTH - 1 - s:CONV_WIDTH - s]
            return out

        for r0 in range(0, tm, FFN_ROW_CHUNK):
            gate = conv(0, r0, cwg_ref, cbg_ref)
            val = conv(1, r0, cwv_ref, cbv_ref)
            act_ref[slot, r0:r0 + FFN_ROW_CHUNK] = ((gate * jax.nn.sigmoid(gate)) * val).astype(BF16)

    def stage_down(slot):
        o_ref[...] += _dot(act_ref[slot], wd_ref[...])

    @pl.when(j == 0)
    def _():
        h = _rms_scale(x_ref[...]) * gain_ref[...] * (1.0 + sc_ref[0]) + sh_ref[0]
        h_ref[...] = h.astype(BF16)
        o_ref[...] = jnp.zeros_like(o_ref)

        @pl.when(i % tiles_per_seq == 0)
        def _():
            carry_ref[...] = jnp.zeros_like(carry_ref)

        stage_up(0)

    @pl.when(j == 1)
    def _():
        stage_act(0)
        stage_up(1)

    for parity in range(2):
        @pl.when((j > 1) & (j < nj) & (j % 2 == parity))
        def _():
            stage_act(1 - parity)
            stage_up(parity)
            stage_down(parity)

    @pl.when(j == nj)
    def _():
        stage_act((nj - 1) % 2)
        stage_down(nj % 2)

    @pl.when(j == nj + 1)
    def _():
        stage_down((nj + 1) % 2)
        y = x_ref[...] + (1.0 + g_ref[0]) * o_ref[...]
        if final_norm:
            y = _rms_scale(y) * fin_ref[...]
        o_ref[...] = y


def _conv_ffn(x, gain, sc, sh, g, w_up, conv_w, conv_b, w_down, final_gain, *, seq, tm=1024, tf=512):
    m, d = x.shape
    d_ff = w_down.shape[0]
    nj = d_ff // tf
    tiles_per_seq = seq // tm
    final_norm = final_gain is not None

    def per_seq(i, j):
        return (i // tiles_per_seq, 0, 0)

    def blk(j, lag):
        return jnp.clip(j - lag, 0, nj - 1)

    in_specs = [
        pl.BlockSpec((tm, d), lambda i, j: (i, 0), pipeline_mode=pl.Buffered(1)),
        pl.BlockSpec((1, d), lambda i, j: (0, 0)),
        pl.BlockSpec((1, 1, d), per_seq),
        pl.BlockSpec((1, 1, d), per_seq),
        pl.BlockSpec((1, 1, d), per_seq),
        pl.BlockSpec((d, tf), lambda i, j: (0, blk(j, 0))),
        pl.BlockSpec((d, tf), lambda i, j: (0, nj + blk(j, 0))),
        pl.BlockSpec((CONV_WIDTH, tf), lambda i, j: (0, blk(j, 1))),
        pl.BlockSpec((CONV_WIDTH, tf), lambda i, j: (0, nj + blk(j, 1))),
        pl.BlockSpec((1, tf), lambda i, j: (0, blk(j, 1))),
        pl.BlockSpec((1, tf), lambda i, j: (0, nj + blk(j, 1))),
        pl.BlockSpec((tf, d), lambda i, j: (blk(j, 2), 0)),
    ]
    args = [x, gain, sc, sh, g, w_up, w_up, conv_w, conv_w, conv_b, conv_b, w_down]
    if final_norm:
        in_specs.append(pl.BlockSpec((1, d), lambda i, j: (0, 0)))
        args.append(final_gain)
    return pl.pallas_call(
        functools.partial(_ffn_kernel, tiles_per_seq=tiles_per_seq, final_norm=final_norm, nj=nj),
        grid=(m // tm, nj + 2),
        in_specs=in_specs,
        out_specs=pl.BlockSpec((tm, d), lambda i, j: (i, 0), pipeline_mode=pl.Buffered(1)),
        out_shape=jax.ShapeDtypeStruct((m, d), F32),
        scratch_shapes=[
            pltpu.VMEM((tm, d), BF16),
            pltpu.VMEM((nj, 2, SUBLANES, tf), F32),
            pltpu.VMEM((2, tm, tf), BF16),
            pltpu.VMEM((2, 2, SUBLANES + tm, tf), F32),
        ],
        compiler_params=_params("arbitrary", "arbitrary"),
        name="conv_ffn",
    )(*args)


def _pad_cols(w, width):
    return jnp.pad(w, ((0, 0), (0, width - w.shape[1])))


def kernel(x, c, w_mod, b_mod, norm_mix, norm_ffn, gla_w_in, gla_w_gate, gla_b_gate, gla_norm, gla_w_out,
           fox_w_in, fox_b_f, fox_q_norm, fox_k_norm, fox_w_out, ffn_w_up, ffn_conv_w, ffn_conv_b,
           ffn_w_down, norm_final):
    bsz, seq, d = x.shape
    depth = w_mod.shape[0]
    m = bsz * seq
    xf = x.reshape(m, d)

    mod = _modulation(c, w_mod, b_mod).reshape(depth, bsz, 6, 1, d)

    for i in range(depth):
        sh_m, sc_m, g_m, sh_f, sc_f, g_f = (mod[i, :, t] for t in range(6))
        j = i // 2
        if i % 2 == 0:
            n_main = gla_w_in.shape[2] - GLA_RANK
            w_in = gla_w_in[j]
            proj, a = _norm_proj(xf, norm_mix[i][None], sc_m, sh_m, w_in[:, :n_main].astype(BF16),
                                 _pad_cols(w_in[:, n_main:], LANES).astype(BF16), seq=seq)
            wg = jnp.pad(gla_w_gate[j], ((0, LANES - GLA_RANK), (0, 0))).astype(BF16)
            o = _gla_core(proj, a, wg, gla_b_gate[j][None], gla_norm[j][None], bsz=bsz, seq=seq)
            w_out = gla_w_out[j]
        else:
            n_main = fox_w_in.shape[2] - FOX_HEADS
            w_in = fox_w_in[j]
            proj, fl = _norm_proj(xf, norm_mix[i][None], sc_m, sh_m, w_in[:, :n_main].astype(BF16),
                                  _pad_cols(w_in[:, n_main:], LANES).astype(BF16), seq=seq)
            ck = _fox_cum(fl, _pad_cols(fox_b_f[j][None], LANES), bsz=bsz, seq=seq)
            o = _fox_attention(proj, ck.reshape(bsz, FOX_HEADS, 1, seq), fox_q_norm[j][None],
                               fox_k_norm[j][None], bsz=bsz, seq=seq)
            w_out = fox_w_out[j]
        xf = _proj_residual(o, w_out.astype(BF16), xf, g_m, seq=seq)
        xf = _conv_ffn(xf, norm_ffn[i][None], sc_f, sh_f, g_f, ffn_w_up[i].astype(BF16), ffn_conv_w[i],
                       ffn_conv_b[i][None], ffn_w_down[i].astype(BF16),
                       norm_final[None] if i == depth - 1 else None, seq=seq)
    return xf.reshape(bsz, seq, d)
```

```python
import functools

import jax
import jax.numpy as jnp
from jax import lax
from jax.experimental import pallas as pl
from jax.experimental.pallas import tpu as pltpu

F32 = jnp.float32
BF16 = jnp.bfloat16

NORM_EPS = 1e-6
GLA_HEADS = 4
GLA_RANK = 16
GLA_TAU = 16.0
GLA_CHUNK = 64
FOX_HEADS = 16
CONV_WIDTH = 3

LANES = 128
SUBLANES = 8
V7X_VMEM_BYTES = 64 * 1024 * 1024
VMEM_LIMIT_BYTES = V7X_VMEM_BYTES - 8 * 1024 * 1024

LOG2_E = 1.4426950408889634

NT_DIMS = (((1,), (1,)), ((), ()))
TN_DIMS = (((0,), (0,)), ((), ()))


def _params(*semantics):
    return pltpu.CompilerParams(dimension_semantics=semantics, vmem_limit_bytes=VMEM_LIMIT_BYTES)


def _dot(a, b):
    return jnp.dot(a, b, preferred_element_type=F32)


def _rms_scale(x):
    return x * lax.rsqrt(jnp.mean(x * x, axis=-1, keepdims=True) + NORM_EPS)


def _log_sigmoid(x):
    return jnp.minimum(x, 0.0) - jnp.log1p(jnp.exp(-jnp.abs(x)))


def _split_bf16(x, parts):
    out = []
    for _ in range(parts):
        hi = x.astype(BF16)
        out.append(hi)
        x = x - hi.astype(F32)
    return out


def _mod_kernel(c_ref, w_ref, b_ref, o_ref):
    c = c_ref[...]
    cond = c * jax.nn.sigmoid(c)
    o_ref[0] = _dot(cond.astype(BF16), w_ref[0].astype(BF16)) + b_ref[0]


def _modulation(c, w_mod, b_mod, *, tn=1024):
    depth, d, n = w_mod.shape
    bsz = c.shape[0]
    return pl.pallas_call(
        _mod_kernel,
        grid=(depth, n // tn),
        in_specs=[
            pl.BlockSpec((bsz, d), lambda l, j: (0, 0)),
            pl.BlockSpec((1, d, tn), lambda l, j: (l, 0, j)),
            pl.BlockSpec((1, 1, tn), lambda l, j: (l, 0, j)),
        ],
        out_specs=pl.BlockSpec((1, bsz, tn), lambda l, j: (l, 0, j)),
        out_shape=jax.ShapeDtypeStruct((depth, bsz, n), F32),
        compiler_params=_params("parallel", "parallel"),
        name="adaln_modulation",
    )(c, w_mod, b_mod.reshape(depth, 1, n))


def _norm_proj_kernel(x_ref, gain_ref, sc_ref, sh_ref, w_ref, ws_ref, o_ref, os_ref, h_ref):
    @pl.when(pl.program_id(1) == 0)
    def _():
        h = _rms_scale(x_ref[...]) * gain_ref[...] * (1.0 + sc_ref[0]) + sh_ref[0]
        h_ref[...] = h.astype(BF16)
        os_ref[...] = _dot(h_ref[...], ws_ref[...])

    o_ref[...] = _dot(h_ref[...], w_ref[...]).astype(o_ref.dtype)


def _norm_proj(x, gain, sc, sh, w, w_small, *, n, seq, tm=1024, tn=1024):
    m, d = x.shape
    assert n % tn == 0 and n <= w.shape[1]
    tiles_per_seq = seq // tm
    return pl.pallas_call(
        _norm_proj_kernel,
        grid=(m // tm, n // tn),
        in_specs=[
            pl.BlockSpec((tm, d), lambda i, j: (i, 0)),
            pl.BlockSpec((1, d), lambda i, j: (0, 0)),
            pl.BlockSpec((1, 1, d), lambda i, j: (i // tiles_per_seq, 0, 0)),
            pl.BlockSpec((1, 1, d), lambda i, j: (i // tiles_per_seq, 0, 0)),
            pl.BlockSpec((d, tn), lambda i, j: (0, j)),
            pl.BlockSpec((d, LANES), lambda i, j: (0, 0)),
        ],
        out_specs=[
            pl.BlockSpec((tm, tn), lambda i, j: (i, j)),
            pl.BlockSpec((tm, LANES), lambda i, j: (i, 0)),
        ],
        out_shape=[
            jax.ShapeDtypeStruct((m, n), BF16),
            jax.ShapeDtypeStruct((m, LANES), F32),
        ],
        scratch_shapes=[pltpu.VMEM((tm, d), BF16)],
        compiler_params=_params("parallel", "arbitrary"),
        name="norm_modulate_in_proj",
    )(x, gain, sc, sh, w, w_small)


def _proj_residual_kernel(a_ref, w_ref, x_ref, g_ref, o_ref):
    o_ref[...] = x_ref[...] + (1.0 + g_ref[0]) * _dot(a_ref[...], w_ref[...])


def _proj_residual(a, w, x, g, *, seq, tm=1024, tn=1024):
    m, k = a.shape
    n = w.shape[1]
    tiles_per_seq = seq // tm
    return pl.pallas_call(
        _proj_residual_kernel,
        grid=(m // tm, n // tn),
        in_specs=[
            pl.BlockSpec((tm, k), lambda i, j: (i, 0)),
            pl.BlockSpec((k, tn), lambda i, j: (0, j)),
            pl.BlockSpec((tm, tn), lambda i, j: (i, j)),
            pl.BlockSpec((1, 1, tn), lambda i, j: (i // tiles_per_seq, 0, j)),
        ],
        out_specs=pl.BlockSpec((tm, tn), lambda i, j: (i, j)),
        out_shape=jax.ShapeDtypeStruct((m, n), F32),
        compiler_params=_params("parallel", "parallel"),
        name="out_proj_residual",
    )(a, w, x, g)


def _gla_kernel(q_ref, k_ref, v_ref, r_ref, a_ref, wg_ref, bg_ref, gn_ref, o_ref, state_ref, *, chunk):
    @pl.when(pl.program_id(2) == 0)
    def _():
        state_ref[...] = jnp.zeros_like(state_ref)

    rows, dk = q_ref.shape
    q_scale = dk ** -0.5
    log_alpha = _log_sigmoid(_dot(a_ref[...].astype(BF16), wg_ref[...]) + bg_ref[...]) / GLA_TAU

    row = lax.broadcasted_iota(jnp.int32, (chunk, chunk), 0)
    col = lax.broadcasted_iota(jnp.int32, (chunk, chunk), 1)
    causal = row >= col
    tri = causal.astype(BF16)

    for c in range(rows // chunk):
        sl = pl.ds(c * chunk, chunk)
        la = log_alpha[c * chunk:(c + 1) * chunk]
        b = sum(_dot(tri, part) for part in _split_bf16(la, 2))
        b_last = b[chunk - 1:chunk]
        q = q_ref[sl, :].astype(F32)
        k = k_ref[sl, :].astype(F32)
        v = v_ref[sl, :]
        q_dec = (q * (q_scale * jnp.exp(b))).astype(BF16)
        k_inv = (k * jnp.exp(-b)).astype(BF16)
        k_end = (k * jnp.exp(b_last - b)).astype(BF16)

        attn = jnp.where(causal, lax.dot_general(q_dec, k_inv, NT_DIMS, preferred_element_type=F32), 0.0)
        state = state_ref[...]
        o = _dot(attn.astype(BF16), v)
        o = o + lax.dot_general(q_dec, state.astype(BF16), NT_DIMS, preferred_element_type=F32)
        state_ref[...] = state * jnp.exp(b_last) + lax.dot_general(v, k_end, TN_DIMS, preferred_element_type=F32)

        r = r_ref[sl, :].astype(F32)
        o = (_rms_scale(o) * gn_ref[...]) * (r * jax.nn.sigmoid(r))
        o_ref[sl, :] = o.astype(o_ref.dtype)


def _gla_core(proj, a, wg, bg, gn, *, bsz, seq, rows=512):
    m = proj.shape[0]
    dk = wg.shape[1] // GLA_HEADS
    dv = gn.shape[1] // GLA_HEADS
    nblk = seq // rows
    k_off = (GLA_HEADS * dk) // dk
    v_off = (2 * GLA_HEADS * dk) // dv
    r_off = v_off + GLA_HEADS

    def row_blk(b, h, t):
        return b * nblk + t

    return pl.pallas_call(
        functools.partial(_gla_kernel, chunk=GLA_CHUNK),
        grid=(bsz, GLA_HEADS, nblk),
        in_specs=[
            pl.BlockSpec((rows, dk), lambda b, h, t: (row_blk(b, h, t), h)),
            pl.BlockSpec((rows, dk), lambda b, h, t: (row_blk(b, h, t), k_off + h)),
            pl.BlockSpec((rows, dv), lambda b, h, t: (row_blk(b, h, t), v_off + h)),
            pl.BlockSpec((rows, dv), lambda b, h, t: (row_blk(b, h, t), r_off + h)),
            pl.BlockSpec((rows, LANES), lambda b, h, t: (row_blk(b, h, t), 0)),
            pl.BlockSpec((LANES, dk), lambda b, h, t: (0, h)),
            pl.BlockSpec((1, dk), lambda b, h, t: (0, h)),
            pl.BlockSpec((1, dv), lambda b, h, t: (0, h)),
        ],
        out_specs=pl.BlockSpec((rows, dv), lambda b, h, t: (row_blk(b, h, t), h)),
        out_shape=jax.ShapeDtypeStruct((m, GLA_HEADS * dv), BF16),
        scratch_shapes=[pltpu.VMEM((dv, dk), F32)],
        compiler_params=_params("parallel", "parallel", "arbitrary"),
        name="gla_chunked",
    )(proj, proj, proj, proj, a, wg, bg, gn)


def _fox_cum_kernel(fl_ref, bf_ref, o_ref, *, blk):
    seq = fl_ref.shape[0]
    log_f = _log_sigmoid(fl_ref[...] + bf_ref[...])
    row = lax.broadcasted_iota(jnp.int32, (blk, blk), 0)
    col = lax.broadcasted_iota(jnp.int32, (blk, blk), 1)
    tri = (row >= col).astype(BF16)
    carry = jnp.zeros((1, LANES), F32)
    pieces = []
    for s in range(seq // blk):
        lf = log_f[s * blk:(s + 1) * blk]
        cum = sum(_dot(tri, part) for part in _split_bf16(lf, 3)) + carry
        carry = cum[blk - 1:blk]
        pieces.append(cum)
    cum = jnp.concatenate(pieces, axis=0)
    o_ref[0] = cum.T[:o_ref.shape[1]]


def _fox_cum(fl, bf, *, bsz, seq, blk=256):
    return pl.pallas_call(
        functools.partial(_fox_cum_kernel, blk=blk),
        grid=(bsz,),
        in_specs=[
            pl.BlockSpec((seq, LANES), lambda b: (b, 0)),
            pl.BlockSpec((1, LANES), lambda b: (0, 0)),
        ],
        out_specs=pl.BlockSpec((1, FOX_HEADS, seq), lambda b: (b, 0, 0)),
        out_shape=jax.ShapeDtypeStruct((bsz, FOX_HEADS, seq), F32),
        compiler_params=_params("parallel"),
        name="fox_cum_log_forget",
    )(fl, bf)


def _fox_attn_kernel(q_ref, k_ref, v_ref, og_ref, ck_ref, qg_ref, kg_ref, o_ref, kn_ref, qn_ref, m_ref, l_ref,
                     acc_ref, *, heads, hd, rs, tk):
    qi = pl.program_id(2)
    tq = q_ref.shape[0]

    @pl.when(qi == 0)
    def _():
        for g in range(heads):
            cols = slice(g * hd, (g + 1) * hd)
            kn_ref[:, cols] = (_rms_scale(k_ref[:, cols].astype(F32)) * kg_ref[...]).astype(BF16)

    m_ref[...] = jnp.full(m_ref.shape, -jnp.inf, F32)
    l_ref[...] = jnp.zeros(l_ref.shape, F32)
    acc_ref[...] = jnp.zeros(acc_ref.shape, F32)

    q_scale = LOG2_E * hd ** -0.5
    for g in range(heads):
        cols = slice(g * hd, (g + 1) * hd)
        qn_ref[:, cols] = ((_rms_scale(q_ref[:, cols].astype(F32)) * qg_ref[...]) * q_scale).astype(BF16)

    def scores(unit):
        g, r, off, _ = unit
        cols = slice(g * hd, (g + 1) * hd)
        kb = kn_ref[pl.ds(off, tk), cols]
        ckb = ck_ref[0, g, :, pl.ds(off, tk)] * LOG2_E
        return lax.dot_general(qn_ref[r * rs:(r + 1) * rs, cols], kb, NT_DIMS, preferred_element_type=F32) - ckb

    def softmax_update(unit, s):
        g, r, _, diag_shift = unit
        rows = slice(r * rs, (r + 1) * rs)
        if diag_shift is not None:
            visible = (lax.broadcasted_iota(jnp.int32, (rs, tk), 1) + diag_shift
                       <= lax.broadcasted_iota(jnp.int32, (rs, tk), 0))
            s = jnp.where(visible, s, -jnp.inf)
        m_prev = m_ref[g, rows]
        m_new = jnp.maximum(m_prev, jnp.max(s, axis=-1, keepdims=True))
        alpha = jnp.exp2(m_prev - m_new)
        p = jnp.exp2(s - jnp.concatenate([m_new] * (tk // LANES), axis=1))
        l_ref[g, rows] = alpha * l_ref[g, rows] + jnp.sum(p, axis=-1, keepdims=True)
        m_ref[g, rows] = m_new
        return p.astype(BF16), alpha

    def weighted_values(unit, p, alpha):
        g, r, off, _ = unit
        rows = slice(r * rs, (r + 1) * rs)
        vb = v_ref[pl.ds(off, tk), g * hd:(g + 1) * hd]
        acc_ref[g, rows] = alpha * acc_ref[g, rows] + _dot(p, vb)

    def run(units, lookahead=1):
        s = {u: scores(units[u]) for u in range(min(lookahead, len(units)))}
        for u, unit in enumerate(units):
            p, alpha = softmax_update(unit, s.pop(u))
            if u + lookahead < len(units):
                s[u + lookahead] = scores(units[u + lookahead])
            weighted_values(unit, p, alpha)

    def body(j, carry):
        off = pl.multiple_of(j * tk, tk)
        run([(g, r, off, None) for g in range(heads) for r in range(tq // rs)])
        return carry

    lax.fori_loop(0, qi * (tq // tk), body, 0)

    diag_units = []
    for c in range(tq // tk):
        for g in range(heads):
            for r in range(tq // rs):
                first_key, first_row = c * tk, r * rs
                if first_key > first_row + rs - 1:
                    continue
                unmasked = first_key + tk - 1 <= first_row
                diag_units.append((g, r, pl.multiple_of(qi * tq + first_key, tk),
                                   None if unmasked else first_key - first_row))
    run(diag_units)

    for g in range(heads):
        cols = slice(g * hd, (g + 1) * hd)
        og = og_ref[:, cols].astype(F32)
        o_ref[:, cols] = ((acc_ref[g] / l_ref[g]) * jax.nn.sigmoid(og)).astype(o_ref.dtype)


def _fox_attention(proj, ck, qg, kg, *, bsz, seq, tq=512, heads=4, rs=128, tk=256):
    m = proj.shape[0]
    hd = qg.shape[1]
    assert hd == LANES, "softmax statistics are kept lane-replicated at the head width"
    nq = seq // tq
    groups = FOX_HEADS // heads
    w = heads * hd
    return pl.pallas_call(
        functools.partial(_fox_attn_kernel, heads=heads, hd=hd, rs=rs, tk=tk),
        grid=(bsz, groups, nq),
        in_specs=[
            pl.BlockSpec((tq, w), lambda b, h, i: (b * nq + i, h)),
            pl.BlockSpec((seq, w), lambda b, h, i: (b, groups + h)),
            pl.BlockSpec((seq, w), lambda b, h, i: (b, 2 * groups + h)),
            pl.BlockSpec((tq, w), lambda b, h, i: (b * nq + i, 3 * groups + h)),
            pl.BlockSpec((1, heads, 1, seq), lambda b, h, i: (b, h, 0, 0)),
            pl.BlockSpec((1, hd), lambda b, h, i: (0, 0)),
            pl.BlockSpec((1, hd), lambda b, h, i: (0, 0)),
        ],
        out_specs=pl.BlockSpec((tq, w), lambda b, h, i: (b * nq + i, h)),
        out_shape=jax.ShapeDtypeStruct((m, FOX_HEADS * hd), BF16),
        scratch_shapes=[
            pltpu.VMEM((seq, w), BF16),
            pltpu.VMEM((tq, w), BF16),
            pltpu.VMEM((heads, tq, LANES), F32),
            pltpu.VMEM((heads, tq, LANES), F32),
            pltpu.VMEM((heads, tq, hd), F32),
        ],
        compiler_params=_params("parallel", "parallel", "arbitrary"),
        name="fox_attention",
    )(proj, proj, proj, proj, ck, qg, kg)


def _shift_rows(u, prev, s):
    rolled = pltpu.roll(u, s, 0)
    head_rows = lax.broadcasted_iota(jnp.int32, prev.shape, 0)
    top = jnp.where(head_rows < s, pltpu.roll(prev, s, 0), rolled[:SUBLANES])
    return jnp.concatenate([top, rolled[SUBLANES:]], axis=0)


def _ffn_kernel(x_ref, gain_ref, sc_ref, sh_ref, g_ref, wg_ref, wv_ref, cwg_ref, cwv_ref, cbg_ref, cbv_ref,
                wd_ref, *rest, tiles_per_seq, final_norm):
    if final_norm:
        fin_ref, o_ref, h_ref, carry_ref = rest
    else:
        o_ref, h_ref, carry_ref = rest
    i = pl.program_id(0)
    j = pl.program_id(1)
    tm = x_ref.shape[0]

    @pl.when(j == 0)
    def _():
        h = _rms_scale(x_ref[...]) * gain_ref[...] * (1.0 + sc_ref[0]) + sh_ref[0]
        h_ref[...] = h.astype(BF16)
        o_ref[...] = jnp.zeros_like(o_ref)

    @pl.when(i % tiles_per_seq == 0)
    def _():
        carry_ref[j] = jnp.zeros(carry_ref.shape[1:], F32)

    h = h_ref[...]

    def conv(u, slot, cw_ref, cb_ref):
        prev = carry_ref[j, slot]
        carry_ref[j, slot] = u[tm - SUBLANES:]
        out = u * cw_ref[CONV_WIDTH - 1:CONV_WIDTH] + cb_ref[...]
        for s in range(1, CONV_WIDTH):
            out = out + _shift_rows(u, prev, s) * cw_ref[CONV_WIDTH - 1 - s:CONV_WIDTH - s]
        return out

    gate = conv(_dot(h, wg_ref[...]), 0, cwg_ref, cbg_ref)
    val = conv(_dot(h, wv_ref[...]), 1, cwv_ref, cbv_ref)
    act = (gate * jax.nn.sigmoid(gate)) * val
    o_ref[...] += _dot(act.astype(BF16), wd_ref[...])

    @pl.when(j == pl.num_programs(1) - 1)
    def _():
        y = x_ref[...] + (1.0 + g_ref[0]) * o_ref[...]
        if final_norm:
            y = _rms_scale(y) * fin_ref[...]
        o_ref[...] = y


def _conv_ffn(x, gain, sc, sh, g, w_up, conv_w, conv_b, w_down, final_gain, *, layer, seq, tm=1024, tf=512):
    m, d = x.shape
    d_ff = w_down.shape[1]
    nj = d_ff // tf
    tiles_per_seq = seq // tm
    final_norm = final_gain is not None

    def per_seq(i, j):
        return (i // tiles_per_seq, 0, 0)

    in_specs = [
        pl.BlockSpec((tm, d), lambda i, j: (i, 0), pipeline_mode=pl.Buffered(1)),
        pl.BlockSpec((1, d), lambda i, j: (0, 0)),
        pl.BlockSpec((1, 1, d), per_seq),
        pl.BlockSpec((1, 1, d), per_seq),
        pl.BlockSpec((1, 1, d), per_seq),
        pl.BlockSpec((None, d, tf), lambda i, j: (layer, 0, j)),
        pl.BlockSpec((None, d, tf), lambda i, j: (layer, 0, nj + j)),
        pl.BlockSpec((None, CONV_WIDTH, tf), lambda i, j: (layer, 0, j)),
        pl.BlockSpec((None, CONV_WIDTH, tf), lambda i, j: (layer, 0, nj + j)),
        pl.BlockSpec((None, 1, tf), lambda i, j: (layer, 0, j)),
        pl.BlockSpec((None, 1, tf), lambda i, j: (layer, 0, nj + j)),
        pl.BlockSpec((None, tf, d), lambda i, j: (layer, j, 0)),
    ]
    args = [x, gain, sc, sh, g, w_up, w_up, conv_w, conv_w, conv_b, conv_b, w_down]
    if final_norm:
        in_specs.append(pl.BlockSpec((1, d), lambda i, j: (0, 0)))
        args.append(final_gain)
    return pl.pallas_call(
        functools.partial(_ffn_kernel, tiles_per_seq=tiles_per_seq, final_norm=final_norm),
        grid=(m // tm, nj),
        in_specs=in_specs,
        out_specs=pl.BlockSpec((tm, d), lambda i, j: (i, 0), pipeline_mode=pl.Buffered(1)),
        out_shape=jax.ShapeDtypeStruct((m, d), F32),
        scratch_shapes=[
            pltpu.VMEM((tm, d), BF16),
            pltpu.VMEM((nj, 2, SUBLANES, tf), F32),
        ],
        compiler_params=_params("arbitrary", "arbitrary"),
        name="conv_ffn",
    )(*args)


def _pad_cols(w, width):
    return jnp.pad(w, ((0, 0), (0, width - w.shape[1])))


def kernel(x, c, w_mod, b_mod, norm_mix, norm_ffn, gla_w_in, gla_w_gate, gla_b_gate, gla_norm, gla_w_out,
           fox_w_in, fox_b_f, fox_q_norm, fox_k_norm, fox_w_out, ffn_w_up, ffn_conv_w, ffn_conv_b,
           ffn_w_down, norm_final):
    bsz, seq, d = x.shape
    depth = w_mod.shape[0]
    m = bsz * seq
    xf = x.reshape(m, d)

    mod = _modulation(c, w_mod, b_mod).reshape(depth, bsz, 6, 1, d)

    gla_w_in16, fox_w_in16 = gla_w_in.astype(BF16), fox_w_in.astype(BF16)
    gla_w_out16, fox_w_out16 = gla_w_out.astype(BF16), fox_w_out.astype(BF16)
    w_up16, w_down16 = ffn_w_up.astype(BF16), ffn_w_down.astype(BF16)
    conv_b = ffn_conv_b[:, None, :]

    for i in range(depth):
        sh_m, sc_m, g_m, sh_f, sc_f, g_f = (mod[i, :, t] for t in range(6))
        j = i // 2
        if i % 2 == 0:
            n_main = gla_w_in.shape[2] - GLA_RANK
            proj, a = _norm_proj(xf, norm_mix[i][None], sc_m, sh_m, gla_w_in16[j],
                                 _pad_cols(gla_w_in16[j, :, n_main:], LANES), n=n_main, seq=seq)
            wg = jnp.pad(gla_w_gate[j], ((0, LANES - GLA_RANK), (0, 0))).astype(BF16)
            o = _gla_core(proj, a, wg, gla_b_gate[j][None], gla_norm[j][None], bsz=bsz, seq=seq)
            w_out = gla_w_out16[j]
        else:
            n_main = fox_w_in.shape[2] - FOX_HEADS
            proj, fl = _norm_proj(xf, norm_mix[i][None], sc_m, sh_m, fox_w_in16[j],
                                  _pad_cols(fox_w_in16[j, :, n_main:], LANES), n=n_main, seq=seq)
            ck = _fox_cum(fl, _pad_cols(fox_b_f[j][None], LANES), bsz=bsz, seq=seq)
            o = _fox_attention(proj, ck.reshape(bsz, FOX_HEADS, 1, seq), fox_q_norm[j][None],
                               fox_k_norm[j][None], bsz=bsz, seq=seq)
            w_out = fox_w_out16[j]
        xf = _proj_residual(o, w_out, xf, g_m, seq=seq)
        xf = _conv_ffn(xf, norm_ffn[i][None], sc_f, sh_f, g_f, w_up16, ffn_conv_w, conv_b, w_down16,
                       norm_final[None] if i == depth - 1 else None, layer=i, seq=seq)
    return xf.reshape(bsz, seq, d)
```

```python
import functools

import jax
import jax.numpy as jnp
from jax import lax
from jax.experimental import pallas as pl
from jax.experimental.pallas import tpu as pltpu

F32 = jnp.float32
BF16 = jnp.bfloat16

NORM_EPS = 1e-6
GLA_HEADS = 4
GLA_RANK = 16
GLA_TAU = 16.0
GLA_CHUNK = 64
FOX_HEADS = 16
CONV_WIDTH = 3
FFN_ROW_CHUNK = 32
NORM_ROW_CHUNK = 32

LANES = 128
SUBLANES = 8
MXU_COLS = 256
V7X_VMEM_BYTES = 64 * 1024 * 1024
VMEM_LIMIT_BYTES = V7X_VMEM_BYTES - 8 * 1024 * 1024

LOG2_E = 1.4426950408889634

NT_DIMS = (((1,), (1,)), ((), ()))
TN_DIMS = (((0,), (0,)), ((), ()))


def _params(*semantics):
    return pltpu.CompilerParams(dimension_semantics=semantics, vmem_limit_bytes=VMEM_LIMIT_BYTES)


def _dot(a, b):
    return jnp.dot(a, b, preferred_element_type=F32)


def _rms_scale(x):
    return x * lax.rsqrt(jnp.mean(x * x, axis=-1, keepdims=True) + NORM_EPS)


def _log_sigmoid(x):
    return jnp.minimum(x, 0.0) - jnp.log1p(jnp.exp(-jnp.abs(x)))


def _split_bf16(x, parts):
    out = []
    for _ in range(parts):
        hi = x.astype(BF16)
        out.append(hi)
        x = x - hi.astype(F32)
    return out


def _mod_kernel(c_ref, w_ref, b_ref, o_ref):
    c = c_ref[...]
    cond = c * jax.nn.sigmoid(c)
    o_ref[0] = _dot(cond.astype(BF16), w_ref[0].astype(BF16)) + b_ref[0]


def _modulation(c, w_mod, b_mod, *, tn=1024):
    depth, d, n = w_mod.shape
    bsz = c.shape[0]
    return pl.pallas_call(
        _mod_kernel,
        grid=(depth, n // tn),
        in_specs=[
            pl.BlockSpec((bsz, d), lambda l, j: (0, 0)),
            pl.BlockSpec((1, d, tn), lambda l, j: (l, 0, j)),
            pl.BlockSpec((1, 1, tn), lambda l, j: (l, 0, j)),
        ],
        out_specs=pl.BlockSpec((1, bsz, tn), lambda l, j: (l, 0, j)),
        out_shape=jax.ShapeDtypeStruct((depth, bsz, n), F32),
        compiler_params=_params("parallel", "parallel"),
        name="adaln_modulation",
    )(c, w_mod, b_mod.reshape(depth, 1, n))


def _norm_proj_kernel(x_ref, gain_ref, sc_ref, sh_ref, w_ref, ws_ref, o_ref, os_ref, h_ref):
    @pl.when(pl.program_id(1) == 0)
    def _():
        h = _rms_scale(x_ref[...]) * gain_ref[...] * (1.0 + sc_ref[0]) + sh_ref[0]
        h_ref[...] = h.astype(BF16)
        os_ref[...] = _dot(h_ref[...], ws_ref[...])

    o_ref[...] = _dot(h_ref[...], w_ref[...]).astype(o_ref.dtype)


def _norm_proj(x, gain, sc, sh, w, w_small, *, n, seq, tm=1024, tn=1024):
    m, d = x.shape
    assert n % tn == 0 and n <= w.shape[1]
    tiles_per_seq = seq // tm
    return pl.pallas_call(
        _norm_proj_kernel,
        grid=(m // tm, n // tn),
        in_specs=[
            pl.BlockSpec((tm, d), lambda i, j: (i, 0)),
            pl.BlockSpec((1, d), lambda i, j: (0, 0)),
            pl.BlockSpec((1, 1, d), lambda i, j: (i // tiles_per_seq, 0, 0)),
            pl.BlockSpec((1, 1, d), lambda i, j: (i // tiles_per_seq, 0, 0)),
            pl.BlockSpec((d, tn), lambda i, j: (0, j)),
            pl.BlockSpec((d, LANES), lambda i, j: (0, 0)),
        ],
        out_specs=[
            pl.BlockSpec((tm, tn), lambda i, j: (i, j)),
            pl.BlockSpec((tm, LANES), lambda i, j: (i, 0)),
        ],
        out_shape=[
            jax.ShapeDtypeStruct((m, n), BF16),
            jax.ShapeDtypeStruct((m, LANES), F32),
        ],
        scratch_shapes=[pltpu.VMEM((tm, d), BF16)],
        compiler_params=_params("parallel", "arbitrary"),
        name="norm_modulate_in_proj",
    )(x, gain, sc, sh, w, w_small)


def _proj_residual_kernel(a_ref, w_ref, x_ref, g_ref, o_ref):
    o_ref[...] = x_ref[...] + (1.0 + g_ref[0]) * _dot(a_ref[...], w_ref[...])


def _proj_residual(a, w, x, g, *, seq, tm=1024, tn=1024):
    m, k = a.shape
    n = w.shape[1]
    tiles_per_seq = seq // tm
    return pl.pallas_call(
        _proj_residual_kernel,
        grid=(m // tm, n // tn),
        in_specs=[
            pl.BlockSpec((tm, k), lambda i, j: (i, 0)),
            pl.BlockSpec((k, tn), lambda i, j: (0, j)),
            pl.BlockSpec((tm, tn), lambda i, j: (i, j)),
            pl.BlockSpec((1, 1, tn), lambda i, j: (i // tiles_per_seq, 0, j)),
        ],
        out_specs=pl.BlockSpec((tm, tn), lambda i, j: (i, j)),
        out_shape=jax.ShapeDtypeStruct((m, n), F32),
        compiler_params=_params("parallel", "parallel"),
        name="out_proj_residual",
    )(a, w, x, g)


def _gla_kernel(q_ref, k_ref, v_ref, r_ref, a_ref, wg_ref, bg_ref, gn_ref, o_ref, state_ref, *, chunk, heads):
    @pl.when(pl.program_id(2) == 0)
    def _():
        state_ref[...] = jnp.zeros_like(state_ref)

    rows = q_ref.shape[0]
    dk = q_ref.shape[1] // heads
    dv = v_ref.shape[1] // heads
    q_scale = dk ** -0.5
    log_alpha = _log_sigmoid(_dot(a_ref[...].astype(BF16), wg_ref[...]) + bg_ref[...]) / GLA_TAU

    row = lax.broadcasted_iota(jnp.int32, (chunk, chunk), 0)
    col = lax.broadcasted_iota(jnp.int32, (chunk, chunk), 1)
    causal = row >= col
    tri = causal.astype(BF16)

    for c in range(rows // chunk):
        sl = pl.ds(c * chunk, chunk)
        for g in range(heads):
            kc = slice(g * dk, (g + 1) * dk)
            vc = slice(g * dv, (g + 1) * dv)
            la = log_alpha[c * chunk:(c + 1) * chunk, kc]
            b = sum(_dot(tri, part) for part in _split_bf16(la, 2))
            b_last = b[chunk - 1:chunk]
            q = q_ref[sl, kc].astype(F32)
            k = k_ref[sl, kc].astype(F32)
            v = v_ref[sl, vc]
            q_dec = (q * (q_scale * jnp.exp(b))).astype(BF16)
            k_inv = (k * jnp.exp(-b)).astype(BF16)
            k_end = (k * jnp.exp(b_last - b)).astype(BF16)

            attn = jnp.where(causal, lax.dot_general(q_dec, k_inv, NT_DIMS, preferred_element_type=F32), 0.0)
            state = state_ref[g]
            o = _dot(attn.astype(BF16), v)
            o = o + lax.dot_general(q_dec, state.astype(BF16), NT_DIMS, preferred_element_type=F32)
            state_ref[g] = state * jnp.exp(b_last) + lax.dot_general(v, k_end, TN_DIMS, preferred_element_type=F32)

            r = r_ref[sl, vc].astype(F32)
            o = (_rms_scale(o) * gn_ref[:, vc]) * (r * jax.nn.sigmoid(r))
            o_ref[sl, vc] = o.astype(o_ref.dtype)


def _gla_core(proj, a, wg, bg, gn, *, bsz, seq, rows=512, heads=2):
    m = proj.shape[0]
    dk = wg.shape[1] // GLA_HEADS
    dv = gn.shape[1] // GLA_HEADS
    nblk = seq // rows
    groups = GLA_HEADS // heads
    wk, wv = heads * dk, heads * dv
    k_off = (GLA_HEADS * dk) // wk
    v_off = (2 * GLA_HEADS * dk) // wv
    r_off = v_off + groups

    def row_blk(b, h, t):
        return b * nblk + t

    return pl.pallas_call(
        functools.partial(_gla_kernel, chunk=GLA_CHUNK, heads=heads),
        grid=(bsz, groups, nblk),
        in_specs=[
            pl.BlockSpec((rows, wk), lambda b, h, t: (row_blk(b, h, t), h)),
            pl.BlockSpec((rows, wk), lambda b, h, t: (row_blk(b, h, t), k_off + h)),
            pl.BlockSpec((rows, wv), lambda b, h, t: (row_blk(b, h, t), v_off + h)),
            pl.BlockSpec((rows, wv), lambda b, h, t: (row_blk(b, h, t), r_off + h)),
            pl.BlockSpec((rows, LANES), lambda b, h, t: (row_blk(b, h, t), 0)),
            pl.BlockSpec((LANES, wk), lambda b, h, t: (0, h)),
            pl.BlockSpec((1, wk), lambda b, h, t: (0, h)),
            pl.BlockSpec((1, wv), lambda b, h, t: (0, h)),
        ],
        out_specs=pl.BlockSpec((rows, wv), lambda b, h, t: (row_blk(b, h, t), h)),
        out_shape=jax.ShapeDtypeStruct((m, GLA_HEADS * dv), BF16),
        scratch_shapes=[pltpu.VMEM((heads, dv, dk), F32)],
        compiler_params=_params("parallel", "parallel", "arbitrary"),
        name="gla_chunked",
    )(proj, proj, proj, proj, a, wg, bg, gn)


def _fox_cum_kernel(fl_ref, bf_ref, o_ref, *, blk):
    seq = fl_ref.shape[0]
    log_f = _log_sigmoid(fl_ref[...] + bf_ref[...])
    row = lax.broadcasted_iota(jnp.int32, (blk, blk), 0)
    col = lax.broadcasted_iota(jnp.int32, (blk, blk), 1)
    tri = (row >= col).astype(BF16)
    carry = jnp.zeros((1, LANES), F32)
    pieces = []
    for s in range(seq // blk):
        lf = log_f[s * blk:(s + 1) * blk]
        cum = sum(_dot(tri, part) for part in _split_bf16(lf, 3)) + carry
        carry = cum[blk - 1:blk]
        pieces.append(cum)
    cum = jnp.concatenate(pieces, axis=0)
    o_ref[0] = cum.T[:o_ref.shape[1]]


def _fox_cum(fl, bf, *, bsz, seq, blk=256):
    return pl.pallas_call(
        functools.partial(_fox_cum_kernel, blk=blk),
        grid=(bsz,),
        in_specs=[
            pl.BlockSpec((seq, LANES), lambda b: (b, 0)),
            pl.BlockSpec((1, LANES), lambda b: (0, 0)),
        ],
        out_specs=pl.BlockSpec((1, FOX_HEADS, seq), lambda b: (b, 0, 0)),
        out_shape=jax.ShapeDtypeStruct((bsz, FOX_HEADS, seq), F32),
        compiler_params=_params("parallel"),
        name="fox_cum_log_forget",
    )(fl, bf)


def _fox_attn_kernel(q_ref, k_ref, v_ref, og_ref, ck_ref, qg_ref, kg_ref, o_ref, kn_ref, qn_ref, m_ref, l_ref,
                     acc_ref, *, heads, hd, rs, tk):
    qi = pl.program_id(2)
    tq = q_ref.shape[0]

    @pl.when(qi == 0)
    def _():
        for g in range(heads):
            cols = slice(g * hd, (g + 1) * hd)
            kn_ref[:, cols] = (_rms_scale(k_ref[:, cols].astype(F32)) * kg_ref[...]).astype(BF16)

    m_ref[...] = jnp.full(m_ref.shape, -jnp.inf, F32)
    l_ref[...] = jnp.zeros(l_ref.shape, F32)
    acc_ref[...] = jnp.zeros(acc_ref.shape, F32)

    q_scale = LOG2_E * hd ** -0.5
    for g in range(heads):
        cols = slice(g * hd, (g + 1) * hd)
        qn_ref[:, cols] = ((_rms_scale(q_ref[:, cols].astype(F32)) * qg_ref[...]) * q_scale).astype(BF16)

    def scores(unit):
        g, r, off, _ = unit
        cols = slice(g * hd, (g + 1) * hd)
        kb = kn_ref[pl.ds(off, tk), cols]
        ckb = ck_ref[0, g, :, pl.ds(off, tk)] * LOG2_E
        return lax.dot_general(qn_ref[r * rs:(r + 1) * rs, cols], kb, NT_DIMS, preferred_element_type=F32) - ckb

    def softmax_update(unit, s):
        g, r, _, diag_shift = unit
        rows = slice(r * rs, (r + 1) * rs)
        if diag_shift is not None:
            visible = (lax.broadcasted_iota(jnp.int32, (rs, tk), 1) + diag_shift
                       <= lax.broadcasted_iota(jnp.int32, (rs, tk), 0))
            s = jnp.where(visible, s, -jnp.inf)
        m_prev = m_ref[g, rows]
        m_new = jnp.maximum(m_prev, jnp.max(s, axis=-1, keepdims=True))
        alpha = jnp.exp2(m_prev - m_new)
        p = jnp.exp2(s - jnp.concatenate([m_new] * (tk // LANES), axis=1))
        l_ref[g, rows] = alpha * l_ref[g, rows] + jnp.sum(p, axis=-1, keepdims=True)
        m_ref[g, rows] = m_new
        return p.astype(BF16), alpha

    def weighted_values(unit, p, alpha):
        g, r, off, _ = unit
        rows = slice(r * rs, (r + 1) * rs)
        vb = v_ref[pl.ds(off, tk), g * hd:(g + 1) * hd]
        acc_ref[g, rows] = alpha * acc_ref[g, rows] + _dot(p, vb)

    def run(units, lookahead=1):
        s = {u: scores(units[u]) for u in range(min(lookahead, len(units)))}
        for u, unit in enumerate(units):
            p, alpha = softmax_update(unit, s.pop(u))
            if u + lookahead < len(units):
                s[u + lookahead] = scores(units[u + lookahead])
            weighted_values(unit, p, alpha)

    def body(j, carry):
        off = pl.multiple_of(j * tk, tk)
        run([(g, r, off, None) for g in range(heads) for r in range(tq // rs)])
        return carry

    lax.fori_loop(0, qi * (tq // tk), body, 0)

    diag_units = []
    for c in range(tq // tk):
        for g in range(heads):
            for r in range(tq // rs):
                first_key, first_row = c * tk, r * rs
                if first_key > first_row + rs - 1:
                    continue
                unmasked = first_key + tk - 1 <= first_row
                diag_units.append((g, r, pl.multiple_of(qi * tq + first_key, tk),
                                   None if unmasked else first_key - first_row))
    run(diag_units)

    for g in range(heads):
        cols = slice(g * hd, (g + 1) * hd)
        og = og_ref[:, cols].astype(F32)
        o_ref[:, cols] = ((acc_ref[g] / l_ref[g]) * jax.nn.sigmoid(og)).astype(o_ref.dtype)


def _fox_attention(proj, ck, qg, kg, *, bsz, seq, tq=512, heads=4, rs=128, tk=256):
    m = proj.shape[0]
    hd = qg.shape[1]
    assert hd == LANES, "softmax statistics are kept lane-replicated at the head width"
    nq = seq // tq
    groups = FOX_HEADS // heads
    w = heads * hd
    return pl.pallas_call(
        functools.partial(_fox_attn_kernel, heads=heads, hd=hd, rs=rs, tk=tk),
        grid=(bsz, groups, nq),
        in_specs=[
            pl.BlockSpec((tq, w), lambda b, h, i: (b * nq + i, h)),
            pl.BlockSpec((seq, w), lambda b, h, i: (b, groups + h)),
            pl.BlockSpec((seq, w), lambda b, h, i: (b, 2 * groups + h)),
            pl.BlockSpec((tq, w), lambda b, h, i: (b * nq + i, 3 * groups + h)),
            pl.BlockSpec((1, heads, 1, seq), lambda b, h, i: (b, h, 0, 0)),
            pl.BlockSpec((1, hd), lambda b, h, i: (0, 0)),
            pl.BlockSpec((1, hd), lambda b, h, i: (0, 0)),
        ],
        out_specs=pl.BlockSpec((tq, w), lambda b, h, i: (b * nq + i, h)),
        out_shape=jax.ShapeDtypeStruct((m, FOX_HEADS * hd), BF16),
        scratch_shapes=[
            pltpu.VMEM((seq, w), BF16),
            pltpu.VMEM((tq, w), BF16),
            pltpu.VMEM((heads, tq, LANES), F32),
            pltpu.VMEM((heads, tq, LANES), F32),
            pltpu.VMEM((heads, tq, hd), F32),
        ],
        compiler_params=_params("parallel", "parallel", "arbitrary"),
        name="fox_attention",
    )(proj, proj, proj, proj, ck, qg, kg)


def _ffn_kernel(x_ref, gain_ref, sc_ref, sh_ref, g_ref, wg_ref, wv_ref, cwg_ref, cwv_ref, cbg_ref, cbv_ref,
                wd_ref, *rest, tiles_per_seq, final_norm):
    if final_norm:
        fin_ref, o_ref, h_ref, carry_ref, u_ref, act_ref = rest
    else:
        o_ref, h_ref, carry_ref, u_ref, act_ref = rest
    i = pl.program_id(0)
    j = pl.program_id(1)
    tm = x_ref.shape[0]

    @pl.when(j == 0)
    def _():
        o_ref[...] = jnp.zeros_like(o_ref)
        for r0 in range(0, tm, NORM_ROW_CHUNK):
            rows = slice(r0, r0 + NORM_ROW_CHUNK)
            h = _rms_scale(x_ref[rows]) * gain_ref[...] * (1.0 + sc_ref[0]) + sh_ref[0]
            h_ref[rows] = h.astype(BF16)

    @pl.when(i % tiles_per_seq == 0)
    def _():
        carry_ref[j] = jnp.zeros(carry_ref.shape[1:], F32)

    h = h_ref[...]

    def conv(k, r0, cols, cw_ref, cb_ref):
        out = cb_ref[:, cols]
        for s in range(CONV_WIDTH):
            rows = slice(SUBLANES - s + r0, SUBLANES - s + r0 + FFN_ROW_CHUNK)
            out = out + u_ref[k, rows, cols] * cw_ref[CONV_WIDTH - 1 - s:CONV_WIDTH - s, cols]
        return out

    col_blocks = [slice(c, c + MXU_COLS) for c in range(0, wg_ref.shape[1], MXU_COLS)]
    for cols in col_blocks:
        for k, w_ref in enumerate((wg_ref, wv_ref)):
            u_ref[k, :SUBLANES, cols] = carry_ref[j, k, :, cols]
            u_ref[k, SUBLANES:, cols] = _dot(h, w_ref[:, cols])
            carry_ref[j, k, :, cols] = u_ref[k, tm:, cols]
        for r0 in range(0, tm, FFN_ROW_CHUNK):
            gate = conv(0, r0, cols, cwg_ref, cbg_ref)
            val = conv(1, r0, cols, cwv_ref, cbv_ref)
            act_ref[r0:r0 + FFN_ROW_CHUNK, cols] = ((gate * jax.nn.sigmoid(gate)) * val).astype(BF16)
    for cols in col_blocks:
        o_ref[...] += _dot(act_ref[:, cols], wd_ref[cols, :])

    @pl.when(j == pl.num_programs(1) - 1)
    def _():
        for r0 in range(0, tm, NORM_ROW_CHUNK):
            rows = slice(r0, r0 + NORM_ROW_CHUNK)
            y = x_ref[rows] + (1.0 + g_ref[0]) * o_ref[rows]
            if final_norm:
                y = _rms_scale(y) * fin_ref[...]
            o_ref[rows] = y


def _conv_ffn(x, gain, sc, sh, g, w_up, conv_w, conv_b, w_down, final_gain, *, layer, seq, tm=1024, tf=512):
    m, d = x.shape
    d_ff = w_down.shape[1]
    nj = d_ff // tf
    tiles_per_seq = seq // tm
    final_norm = final_gain is not None

    def per_seq(i, j):
        return (i // tiles_per_seq, 0, 0)

    in_specs = [
        pl.BlockSpec((tm, d), lambda i, j: (i, 0), pipeline_mode=pl.Buffered(1)),
        pl.BlockSpec((1, d), lambda i, j: (0, 0)),
        pl.BlockSpec((1, 1, d), per_seq),
        pl.BlockSpec((1, 1, d), per_seq),
        pl.BlockSpec((1, 1, d), per_seq),
        pl.BlockSpec((None, d, tf), lambda i, j: (layer, 0, j)),
        pl.BlockSpec((None, d, tf), lambda i, j: (layer, 0, nj + j)),
        pl.BlockSpec((None, CONV_WIDTH, tf), lambda i, j: (layer, 0, j)),
        pl.BlockSpec((None, CONV_WIDTH, tf), lambda i, j: (layer, 0, nj + j)),
        pl.BlockSpec((None, 1, tf), lambda i, j: (layer, 0, j)),
        pl.BlockSpec((None, 1, tf), lambda i, j: (layer, 0, nj + j)),
        pl.BlockSpec((None, tf, d), lambda i, j: (layer, j, 0)),
    ]
    args = [x, gain, sc, sh, g, w_up, w_up, conv_w, conv_w, conv_b, conv_b, w_down]
    if final_norm:
        in_specs.append(pl.BlockSpec((1, d), lambda i, j: (0, 0)))
        args.append(final_gain)
    return pl.pallas_call(
        functools.partial(_ffn_kernel, tiles_per_seq=tiles_per_seq, final_norm=final_norm),
        grid=(m // tm, nj),
        in_specs=in_specs,
        out_specs=pl.BlockSpec((tm, d), lambda i, j: (i, 0), pipeline_mode=pl.Buffered(1)),
        out_shape=jax.ShapeDtypeStruct((m, d), F32),
        scratch_shapes=[
            pltpu.VMEM((tm, d), BF16),
            pltpu.VMEM((nj, 2, SUBLANES, tf), F32),
            pltpu.VMEM((2, SUBLANES + tm, tf), F32),
            pltpu.VMEM((tm, tf), BF16),
        ],
        compiler_params=_params("arbitrary", "arbitrary"),
        name="conv_ffn",
    )(*args)


def _pad_cols(w, width):
    return jnp.pad(w, ((0, 0), (0, width - w.shape[1])))


def kernel(x, c, w_mod, b_mod, norm_mix, norm_ffn, gla_w_in, gla_w_gate, gla_b_gate, gla_norm, gla_w_out,
           fox_w_in, fox_b_f, fox_q_norm, fox_k_norm, fox_w_out, ffn_w_up, ffn_conv_w, ffn_conv_b,
           ffn_w_down, norm_final):
    bsz, seq, d = x.shape
    depth = w_mod.shape[0]
    m = bsz * seq
    xf = x.reshape(m, d)

    mod = _modulation(c, w_mod, b_mod).reshape(depth, bsz, 6, 1, d)

    gla_w_in16, fox_w_in16 = gla_w_in.astype(BF16), fox_w_in.astype(BF16)
    gla_w_out16, fox_w_out16 = gla_w_out.astype(BF16), fox_w_out.astype(BF16)
    w_up16, w_down16 = ffn_w_up.astype(BF16), ffn_w_down.astype(BF16)
    conv_b = ffn_conv_b[:, None, :]

    for i in range(depth):
        sh_m, sc_m, g_m, sh_f, sc_f, g_f = (mod[i, :, t] for t in range(6))
        j = i // 2
        if i % 2 == 0:
            n_main = gla_w_in.shape[2] - GLA_RANK
            proj, a = _norm_proj(xf, norm_mix[i][None], sc_m, sh_m, gla_w_in16[j],
                                 _pad_cols(gla_w_in16[j, :, n_main:], LANES), n=n_main, seq=seq)
            wg = jnp.pad(gla_w_gate[j], ((0, LANES - GLA_RANK), (0, 0))).astype(BF16)
            o = _gla_core(proj, a, wg, gla_b_gate[j][None], gla_norm[j][None], bsz=bsz, seq=seq)
            w_out = gla_w_out16[j]
        else:
            n_main = fox_w_in.shape[2] - FOX_HEADS
            proj, fl = _norm_proj(xf, norm_mix[i][None], sc_m, sh_m, fox_w_in16[j],
                                  _pad_cols(fox_w_in16[j, :, n_main:], LANES), n=n_main, seq=seq)
            ck = _fox_cum(fl, _pad_cols(fox_b_f[j][None], LANES), bsz=bsz, seq=seq)
            o = _fox_attention(proj, ck.reshape(bsz, FOX_HEADS, 1, seq), fox_q_norm[j][None],
                               fox_k_norm[j][None], bsz=bsz, seq=seq)
            w_out = fox_w_out16[j]
        xf = _proj_residual(o, w_out, xf, g_m, seq=seq)
        xf = _conv_ffn(xf, norm_ffn[i][None], sc_f, sh_f, g_f, w_up16, ffn_conv_w, conv_b, w_down16,
                       norm_final[None] if i == depth - 1 else None, layer=i, seq=seq)
    return xf.reshape(bsz, seq, d)
```

```python
import functools

import jax
import jax.numpy as jnp
from jax import lax
from jax.experimental import pallas as pl
from jax.experimental.pallas import tpu as pltpu

F32 = jnp.float32
BF16 = jnp.bfloat16

NORM_EPS = 1e-6
GLA_HEADS = 4
GLA_RANK = 16
GLA_TAU = 16.0
GLA_CHUNK = 64
FOX_HEADS = 16
CONV_WIDTH = 3
FFN_ROW_CHUNK = 32
FFN_DOT_ROWS = 512
NORM_ROW_CHUNK = 32

LANES = 128
SUBLANES = 8
MXU_COLS = 256
V7X_VMEM_BYTES = 64 * 1024 * 1024
VMEM_LIMIT_BYTES = V7X_VMEM_BYTES - 8 * 1024 * 1024

LOG2_E = 1.4426950408889634

NT_DIMS = (((1,), (1,)), ((), ()))
TN_DIMS = (((0,), (0,)), ((), ()))


def _params(*semantics):
    return pltpu.CompilerParams(dimension_semantics=semantics, vmem_limit_bytes=VMEM_LIMIT_BYTES)


def _dot(a, b):
    return jnp.dot(a, b, preferred_element_type=F32)


def _rms_scale(x):
    return x * lax.rsqrt(jnp.mean(x * x, axis=-1, keepdims=True) + NORM_EPS)


def _log_sigmoid(x):
    return jnp.minimum(x, 0.0) - jnp.log1p(jnp.exp(-jnp.abs(x)))


def _split_bf16(x, parts):
    out = []
    for _ in range(parts):
        hi = x.astype(BF16)
        out.append(hi)
        x = x - hi.astype(F32)
    return out


def _mod_kernel(c_ref, w_ref, b_ref, o_ref):
    c = c_ref[...]
    cond = c * jax.nn.sigmoid(c)
    o_ref[0] = _dot(cond.astype(BF16), w_ref[0].astype(BF16)) + b_ref[0]


def _modulation(c, w_mod, b_mod, *, tn=1024):
    depth, d, n = w_mod.shape
    bsz = c.shape[0]
    return pl.pallas_call(
        _mod_kernel,
        grid=(depth, n // tn),
        in_specs=[
            pl.BlockSpec((bsz, d), lambda l, j: (0, 0)),
            pl.BlockSpec((1, d, tn), lambda l, j: (l, 0, j)),
            pl.BlockSpec((1, 1, tn), lambda l, j: (l, 0, j)),
        ],
        out_specs=pl.BlockSpec((1, bsz, tn), lambda l, j: (l, 0, j)),
        out_shape=jax.ShapeDtypeStruct((depth, bsz, n), F32),
        compiler_params=_params("parallel", "parallel"),
        name="adaln_modulation",
    )(c, w_mod, b_mod.reshape(depth, 1, n))


def _norm_proj_kernel(x_ref, gain_ref, sc_ref, sh_ref, w_ref, ws_ref, *rest, n_cast):
    cast_src = rest[:n_cast]
    o_ref, os_ref = rest[n_cast:n_cast + 2]
    cast_dst = rest[n_cast + 2:2 * n_cast + 2]
    h_ref = rest[-1]

    @pl.when(pl.program_id(1) == 0)
    def _():
        h = _rms_scale(x_ref[...]) * gain_ref[...] * (1.0 + sc_ref[0]) + sh_ref[0]
        h_ref[...] = h.astype(BF16)
        os_ref[...] = _dot(h_ref[...], ws_ref[...])

    o_ref[...] = _dot(h_ref[...], w_ref[...]).astype(o_ref.dtype)

    for src, dst in zip(cast_src, cast_dst):
        dst[...] = src[...].astype(BF16)


def _cast_block_rows(rows, steps):
    bf16_rows = 2 * SUBLANES
    for r in range(bf16_rows, rows + 1, bf16_rows):
        if rows % r == 0 and rows // r <= steps:
            return r
    raise ValueError("weight does not fit the side cast")


def _norm_proj(x, gain, sc, sh, w, w_small, casts, *, n, seq, tm=1024, tn=1024):
    m, d = x.shape
    assert n % tn == 0 and n <= w.shape[1]
    tiles_per_seq = seq // tm
    ni, nj = m // tm, n // tn
    cast_specs_in, cast_specs_out, cast_shapes = [], [], []
    for stacked, layer in casts:
        _, rows, cols = stacked.shape
        r = _cast_block_rows(rows, ni * nj)
        last = rows // r - 1
        cast_specs_in.append(pl.BlockSpec(
            (None, r, cols), lambda i, j, layer=layer, last=last: (layer, jnp.minimum(i * nj + j, last), 0)))
        cast_specs_out.append(pl.BlockSpec(
            (r, cols), lambda i, j, last=last: (jnp.minimum(i * nj + j, last), 0)))
        cast_shapes.append(jax.ShapeDtypeStruct((rows, cols), BF16))
    outs = pl.pallas_call(
        functools.partial(_norm_proj_kernel, n_cast=len(casts)),
        grid=(ni, nj),
        in_specs=[
            pl.BlockSpec((tm, d), lambda i, j: (i, 0)),
            pl.BlockSpec((1, d), lambda i, j: (0, 0)),
            pl.BlockSpec((1, 1, d), lambda i, j: (i // tiles_per_seq, 0, 0)),
            pl.BlockSpec((1, 1, d), lambda i, j: (i // tiles_per_seq, 0, 0)),
            pl.BlockSpec((d, tn), lambda i, j: (0, j)),
            pl.BlockSpec((d, LANES), lambda i, j: (0, 0)),
        ] + cast_specs_in,
        out_specs=[
            pl.BlockSpec((tm, tn), lambda i, j: (i, j)),
            pl.BlockSpec((tm, LANES), lambda i, j: (i, 0)),
        ] + cast_specs_out,
        out_shape=[
            jax.ShapeDtypeStruct((m, n), BF16),
            jax.ShapeDtypeStruct((m, LANES), F32),
        ] + cast_shapes,
        scratch_shapes=[pltpu.VMEM((tm, d), BF16)],
        compiler_params=_params("arbitrary", "arbitrary"),
        name="norm_modulate_in_proj",
    )(x, gain, sc, sh, w, w_small, *[stacked for stacked, _ in casts])
    return outs[0], outs[1], outs[2:]


def _proj_residual_kernel(a_ref, w_ref, x_ref, g_ref, o_ref):
    o_ref[...] = x_ref[...] + (1.0 + g_ref[0]) * _dot(a_ref[...], w_ref[...])


def _proj_residual(a, w, x, g, *, seq, tm=1024, tn=1024):
    m, k = a.shape
    n = w.shape[1]
    tiles_per_seq = seq // tm
    return pl.pallas_call(
        _proj_residual_kernel,
        grid=(m // tm, n // tn),
        in_specs=[
            pl.BlockSpec((tm, k), lambda i, j: (i, 0)),
            pl.BlockSpec((k, tn), lambda i, j: (0, j)),
            pl.BlockSpec((tm, tn), lambda i, j: (i, j)),
            pl.BlockSpec((1, 1, tn), lambda i, j: (i // tiles_per_seq, 0, j)),
        ],
        out_specs=pl.BlockSpec((tm, tn), lambda i, j: (i, j)),
        out_shape=jax.ShapeDtypeStruct((m, n), F32),
        compiler_params=_params("parallel", "parallel"),
        name="out_proj_residual",
    )(a, w, x, g)


def _gla_kernel(q_ref, k_ref, v_ref, r_ref, a_ref, wg_ref, bg_ref, gn_ref, o_ref, state_ref, *, chunk, heads):
    @pl.when(pl.program_id(2) == 0)
    def _():
        state_ref[...] = jnp.zeros_like(state_ref)

    rows = q_ref.shape[0]
    dk = q_ref.shape[1] // heads
    dv = v_ref.shape[1] // heads
    q_scale = dk ** -0.5
    log_alpha = _log_sigmoid(_dot(a_ref[...].astype(BF16), wg_ref[...]) + bg_ref[...]) / GLA_TAU

    row = lax.broadcasted_iota(jnp.int32, (chunk, chunk), 0)
    col = lax.broadcasted_iota(jnp.int32, (chunk, chunk), 1)
    causal = row >= col
    tri = causal.astype(BF16)

    for c in range(rows // chunk):
        sl = pl.ds(c * chunk, chunk)
        for g in range(heads):
            kc = slice(g * dk, (g + 1) * dk)
            vc = slice(g * dv, (g + 1) * dv)
            la = log_alpha[c * chunk:(c + 1) * chunk, kc]
            b = sum(_dot(tri, part) for part in _split_bf16(la, 2))
            b_last = b[chunk - 1:chunk]
            q = q_ref[sl, kc].astype(F32)
            k = k_ref[sl, kc].astype(F32)
            v = v_ref[sl, vc]
            q_dec = (q * (q_scale * jnp.exp(b))).astype(BF16)
            k_inv = (k * jnp.exp(-b)).astype(BF16)
            k_end = (k * jnp.exp(b_last - b)).astype(BF16)

            attn = jnp.where(causal, lax.dot_general(q_dec, k_inv, NT_DIMS, preferred_element_type=F32), 0.0)
            state = state_ref[g]
            o = _dot(attn.astype(BF16), v)
            o = o + lax.dot_general(q_dec, state.astype(BF16), NT_DIMS, preferred_element_type=F32)
            state_ref[g] = state * jnp.exp(b_last) + lax.dot_general(v, k_end, TN_DIMS, preferred_element_type=F32)

            r = r_ref[sl, vc].astype(F32)
            o = (_rms_scale(o) * gn_ref[:, vc]) * (r * jax.nn.sigmoid(r))
            o_ref[sl, vc] = o.astype(o_ref.dtype)


def _gla_core(proj, a, wg, bg, gn, *, bsz, seq, rows=512, heads=2):
    m = proj.shape[0]
    dk = wg.shape[1] // GLA_HEADS
    dv = gn.shape[1] // GLA_HEADS
    nblk = seq // rows
    groups = GLA_HEADS // heads
    wk, wv = heads * dk, heads * dv
    k_off = (GLA_HEADS * dk) // wk
    v_off = (2 * GLA_HEADS * dk) // wv
    r_off = v_off + groups

    def row_blk(b, h, t):
        return b * nblk + t

    return pl.pallas_call(
        functools.partial(_gla_kernel, chunk=GLA_CHUNK, heads=heads),
        grid=(bsz, groups, nblk),
        in_specs=[
            pl.BlockSpec((rows, wk), lambda b, h, t: (row_blk(b, h, t), h)),
            pl.BlockSpec((rows, wk), lambda b, h, t: (row_blk(b, h, t), k_off + h)),
            pl.BlockSpec((rows, wv), lambda b, h, t: (row_blk(b, h, t), v_off + h)),
            pl.BlockSpec((rows, wv), lambda b, h, t: (row_blk(b, h, t), r_off + h)),
            pl.BlockSpec((rows, LANES), lambda b, h, t: (row_blk(b, h, t), 0)),
            pl.BlockSpec((LANES, wk), lambda b, h, t: (0, h)),
            pl.BlockSpec((1, wk), lambda b, h, t: (0, h)),
            pl.BlockSpec((1, wv), lambda b, h, t: (0, h)),
        ],
        out_specs=pl.BlockSpec((rows, wv), lambda b, h, t: (row_blk(b, h, t), h)),
        out_shape=jax.ShapeDtypeStruct((m, GLA_HEADS * dv), BF16),
        scratch_shapes=[pltpu.VMEM((heads, dv, dk), F32)],
        compiler_params=_params("parallel", "parallel", "arbitrary"),
        name="gla_chunked",
    )(proj, proj, proj, proj, a, wg, bg, gn)


def _fox_cum_kernel(fl_ref, bf_ref, o_ref, *, blk):
    seq = fl_ref.shape[0]
    log_f = _log_sigmoid(fl_ref[...] + bf_ref[...])
    row = lax.broadcasted_iota(jnp.int32, (blk, blk), 0)
    col = lax.broadcasted_iota(jnp.int32, (blk, blk), 1)
    tri = (row >= col).astype(BF16)
    carry = jnp.zeros((1, LANES), F32)
    pieces = []
    for s in range(seq // blk):
        lf = log_f[s * blk:(s + 1) * blk]
        cum = sum(_dot(tri, part) for part in _split_bf16(lf, 3)) + carry
        carry = cum[blk - 1:blk]
        pieces.append(cum)
    cum = jnp.concatenate(pieces, axis=0)
    o_ref[0] = cum.T[:o_ref.shape[1]]


def _fox_cum(fl, bf, *, bsz, seq, blk=256):
    return pl.pallas_call(
        functools.partial(_fox_cum_kernel, blk=blk),
        grid=(bsz,),
        in_specs=[
            pl.BlockSpec((seq, LANES), lambda b: (b, 0)),
            pl.BlockSpec((1, LANES), lambda b: (0, 0)),
        ],
        out_specs=pl.BlockSpec((1, FOX_HEADS, seq), lambda b: (b, 0, 0)),
        out_shape=jax.ShapeDtypeStruct((bsz, FOX_HEADS, seq), F32),
        compiler_params=_params("parallel"),
        name="fox_cum_log_forget",
    )(fl, bf)


def _fox_attn_kernel(q_ref, k_ref, v_ref, og_ref, ck_ref, qg_ref, kg_ref, o_ref, kn_ref, qn_ref, m_ref, l_ref,
                     acc_ref, *, heads, hd, rs, tk):
    qi = pl.program_id(2)
    tq = q_ref.shape[0]

    @pl.when(qi == 0)
    def _():
        for g in range(heads):
            cols = slice(g * hd, (g + 1) * hd)
            kn_ref[:, cols] = (_rms_scale(k_ref[:, cols].astype(F32)) * kg_ref[...]).astype(BF16)

    m_ref[...] = jnp.full(m_ref.shape, -jnp.inf, F32)
    l_ref[...] = jnp.zeros(l_ref.shape, F32)
    acc_ref[...] = jnp.zeros(acc_ref.shape, F32)

    q_scale = LOG2_E * hd ** -0.5
    for g in range(heads):
        cols = slice(g * hd, (g + 1) * hd)
        qn_ref[:, cols] = ((_rms_scale(q_ref[:, cols].astype(F32)) * qg_ref[...]) * q_scale).astype(BF16)

    def scores(unit):
        g, r, off, _ = unit
        cols = slice(g * hd, (g + 1) * hd)
        kb = kn_ref[pl.ds(off, tk), cols]
        ckb = ck_ref[0, g, :, pl.ds(off, tk)] * LOG2_E
        return lax.dot_general(qn_ref[r * rs:(r + 1) * rs, cols], kb, NT_DIMS, preferred_element_type=F32) - ckb

    def softmax_update(unit, s):
        g, r, _, diag_shift = unit
        rows = slice(r * rs, (r + 1) * rs)
        if diag_shift is not None:
            visible = (lax.broadcasted_iota(jnp.int32, (rs, tk), 1) + diag_shift
                       <= lax.broadcasted_iota(jnp.int32, (rs, tk), 0))
            s = jnp.where(visible, s, -jnp.inf)
        m_prev = m_ref[g, rows]
        m_new = jnp.maximum(m_prev, jnp.max(s, axis=-1, keepdims=True))
        alpha = jnp.exp2(m_prev - m_new)
        p = jnp.exp2(s - jnp.concatenate([m_new] * (tk // LANES), axis=1))
        l_ref[g, rows] = alpha * l_ref[g, rows] + jnp.sum(p, axis=-1, keepdims=True)
        m_ref[g, rows] = m_new
        return p.astype(BF16), alpha

    def weighted_values(unit, p, alpha):
        g, r, off, _ = unit
        rows = slice(r * rs, (r + 1) * rs)
        vb = v_ref[pl.ds(off, tk), g * hd:(g + 1) * hd]
        acc_ref[g, rows] = alpha * acc_ref[g, rows] + _dot(p, vb)

    def run(units, lookahead=1):
        s = {u: scores(units[u]) for u in range(min(lookahead, len(units)))}
        for u, unit in enumerate(units):
            p, alpha = softmax_update(unit, s.pop(u))
            if u + lookahead < len(units):
                s[u + lookahead] = scores(units[u + lookahead])
            weighted_values(unit, p, alpha)

    def body(j, carry):
        off = pl.multiple_of(j * tk, tk)
        run([(g, r, off, None) for g in range(heads) for r in range(tq // rs)])
        return carry

    lax.fori_loop(0, qi * (tq // tk), body, 0)

    diag_units = []
    for c in range(tq // tk):
        for g in range(heads):
            for r in range(tq // rs):
                first_key, first_row = c * tk, r * rs
                if first_key > first_row + rs - 1:
                    continue
                unmasked = first_key + tk - 1 <= first_row
                diag_units.append((g, r, pl.multiple_of(qi * tq + first_key, tk),
                                   None if unmasked else first_key - first_row))
    run(diag_units)

    for g in range(heads):
        cols = slice(g * hd, (g + 1) * hd)
        og = og_ref[:, cols].astype(F32)
        o_ref[:, cols] = ((acc_ref[g] / l_ref[g]) * jax.nn.sigmoid(og)).astype(o_ref.dtype)


def _fox_attention(proj, ck, qg, kg, *, bsz, seq, tq=512, heads=4, rs=128, tk=256):
    m = proj.shape[0]
    hd = qg.shape[1]
    assert hd == LANES, "softmax statistics are kept lane-replicated at the head width"
    nq = seq // tq
    groups = FOX_HEADS // heads
    w = heads * hd
    return pl.pallas_call(
        functools.partial(_fox_attn_kernel, heads=heads, hd=hd, rs=rs, tk=tk),
        grid=(bsz, groups, nq),
        in_specs=[
            pl.BlockSpec((tq, w), lambda b, h, i: (b * nq + i, h)),
            pl.BlockSpec((seq, w), lambda b, h, i: (b, groups + h)),
            pl.BlockSpec((seq, w), lambda b, h, i: (b, 2 * groups + h)),
            pl.BlockSpec((tq, w), lambda b, h, i: (b * nq + i, 3 * groups + h)),
            pl.BlockSpec((1, heads, 1, seq), lambda b, h, i: (b, h, 0, 0)),
            pl.BlockSpec((1, hd), lambda b, h, i: (0, 0)),
            pl.BlockSpec((1, hd), lambda b, h, i: (0, 0)),
        ],
        out_specs=pl.BlockSpec((tq, w), lambda b, h, i: (b * nq + i, h)),
        out_shape=jax.ShapeDtypeStruct((m, FOX_HEADS * hd), BF16),
        scratch_shapes=[
            pltpu.VMEM((seq, w), BF16),
            pltpu.VMEM((tq, w), BF16),
            pltpu.VMEM((heads, tq, LANES), F32),
            pltpu.VMEM((heads, tq, LANES), F32),
            pltpu.VMEM((heads, tq, hd), F32),
        ],
        compiler_params=_params("parallel", "parallel", "arbitrary"),
        name="fox_attention",
    )(proj, proj, proj, proj, ck, qg, kg)


def _ffn_kernel(x_ref, gain_ref, sc_ref, sh_ref, g_ref, wg_ref, wv_ref, cwg_ref, cwv_ref, cbg_ref, cbv_ref,
                wd_ref, *rest, tiles_per_seq, final_norm):
    if final_norm:
        fin_ref, o_ref, h_ref, carry_ref, u_ref, act_ref = rest
    else:
        o_ref, h_ref, carry_ref, u_ref, act_ref = rest
    i = pl.program_id(0)
    j = pl.program_id(1)
    tm = x_ref.shape[0]

    @pl.when(j == 0)
    def _():
        o_ref[...] = jnp.zeros_like(o_ref)
        for r0 in range(0, tm, NORM_ROW_CHUNK):
            rows = slice(r0, r0 + NORM_ROW_CHUNK)
            h = _rms_scale(x_ref[rows]) * gain_ref[...] * (1.0 + sc_ref[0]) + sh_ref[0]
            h_ref[rows] = h.astype(BF16)

    @pl.when(i % tiles_per_seq == 0)
    def _():
        carry_ref[j] = jnp.zeros(carry_ref.shape[1:], F32)

    h = h_ref[...]

    def conv(k, r0, cols, cw_ref, cb_ref):
        out = cb_ref[:, cols]
        for s in range(CONV_WIDTH):
            rows = slice(SUBLANES - s + r0, SUBLANES - s + r0 + FFN_ROW_CHUNK)
            out = out + u_ref[k, rows, cols] * cw_ref[CONV_WIDTH - 1 - s:CONV_WIDTH - s, cols]
        return out

    col_blocks = [slice(c, c + MXU_COLS) for c in range(0, wg_ref.shape[1], MXU_COLS)]
    for cols in col_blocks:
        for k, w_ref in enumerate((wg_ref, wv_ref)):
            u_ref[k, :SUBLANES, cols] = carry_ref[j, k, :, cols]
            for m0 in range(0, tm, FFN_DOT_ROWS):
                u_ref[k, SUBLANES + m0:SUBLANES + m0 + FFN_DOT_ROWS, cols] = _dot(
                    h_ref[m0:m0 + FFN_DOT_ROWS], w_ref[:, cols])
            carry_ref[j, k, :, cols] = u_ref[k, tm:, cols]
        for r0 in range(0, tm, FFN_ROW_CHUNK):
            gate = conv(0, r0, cols, cwg_ref, cbg_ref)
            val = conv(1, r0, cols, cwv_ref, cbv_ref)
            act_ref[r0:r0 + FFN_ROW_CHUNK, cols] = ((gate * jax.nn.sigmoid(gate)) * val).astype(BF16)
    o_ref[...] += _dot(act_ref[...], wd_ref[...])

    @pl.when(j == pl.num_programs(1) - 1)
    def _():
        for r0 in range(0, tm, NORM_ROW_CHUNK):
            rows = slice(r0, r0 + NORM_ROW_CHUNK)
            y = x_ref[rows] + (1.0 + g_ref[0]) * o_ref[rows]
            if final_norm:
                y = _rms_scale(y) * fin_ref[...]
            o_ref[rows] = y


def _conv_ffn(x, gain, sc, sh, g, w_up, conv_w, conv_b, w_down, final_gain, *, layer, seq, tm=1024, tf=512):
    m, d = x.shape
    d_ff = w_down.shape[0]
    nj = d_ff // tf
    tiles_per_seq = seq // tm
    final_norm = final_gain is not None

    def per_seq(i, j):
        return (i // tiles_per_seq, 0, 0)

    in_specs = [
        pl.BlockSpec((tm, d), lambda i, j: (i, 0), pipeline_mode=pl.Buffered(1)),
        pl.BlockSpec((1, d), lambda i, j: (0, 0)),
        pl.BlockSpec((1, 1, d), per_seq),
        pl.BlockSpec((1, 1, d), per_seq),
        pl.BlockSpec((1, 1, d), per_seq),
        pl.BlockSpec((d, tf), lambda i, j: (0, j)),
        pl.BlockSpec((d, tf), lambda i, j: (0, nj + j)),
        pl.BlockSpec((None, CONV_WIDTH, tf), lambda i, j: (layer, 0, j)),
        pl.BlockSpec((None, CONV_WIDTH, tf), lambda i, j: (layer, 0, nj + j)),
        pl.BlockSpec((None, 1, tf), lambda i, j: (layer, 0, j)),
        pl.BlockSpec((None, 1, tf), lambda i, j: (layer, 0, nj + j)),
        pl.BlockSpec((tf, d), lambda i, j: (j, 0)),
    ]
    args = [x, gain, sc, sh, g, w_up, w_up, conv_w, conv_w, conv_b, conv_b, w_down]
    if final_norm:
        in_specs.append(pl.BlockSpec((1, d), lambda i, j: (0, 0)))
        args.append(final_gain)
    return pl.pallas_call(
        functools.partial(_ffn_kernel, tiles_per_seq=tiles_per_seq, final_norm=final_norm),
        grid=(m // tm, nj),
        in_specs=in_specs,
        out_specs=pl.BlockSpec((tm, d), lambda i, j: (i, 0), pipeline_mode=pl.Buffered(1)),
        out_shape=jax.ShapeDtypeStruct((m, d), F32),
        scratch_shapes=[
            pltpu.VMEM((tm, d), BF16),
            pltpu.VMEM((nj, 2, SUBLANES, tf), F32),
            pltpu.VMEM((2, SUBLANES + tm, tf), F32),
            pltpu.VMEM((tm, tf), BF16),
        ],
        compiler_params=_params("arbitrary", "arbitrary"),
        name="conv_ffn",
    )(*args)


def _pad_cols(w, width):
    return jnp.pad(w, ((0, 0), (0, width - w.shape[1])))


def kernel(x, c, w_mod, b_mod, norm_mix, norm_ffn, gla_w_in, gla_w_gate, gla_b_gate, gla_norm, gla_w_out,
           fox_w_in, fox_b_f, fox_q_norm, fox_k_norm, fox_w_out, ffn_w_up, ffn_conv_w, ffn_conv_b,
           ffn_w_down, norm_final):
    bsz, seq, d = x.shape
    depth = w_mod.shape[0]
    m = bsz * seq
    xf = x.reshape(m, d)

    mod = _modulation(c, w_mod, b_mod).reshape(depth, bsz, 6, 1, d)

    gla_w_in16, fox_w_in16 = gla_w_in.astype(BF16), fox_w_in.astype(BF16)
    conv_b = ffn_conv_b[:, None, :]

    for i in range(depth):
        sh_m, sc_m, g_m, sh_f, sc_f, g_f = (mod[i, :, t] for t in range(6))
        j = i // 2
        if i % 2 == 0:
            n_main = gla_w_in.shape[2] - GLA_RANK
            proj, a, (w_up16, w_down16, w_out16) = _norm_proj(
                xf, norm_mix[i][None], sc_m, sh_m, gla_w_in16[j], _pad_cols(gla_w_in16[j, :, n_main:], LANES),
                [(ffn_w_up, i), (ffn_w_down, i), (gla_w_out, j)], n=n_main, seq=seq)
            wg = jnp.pad(gla_w_gate[j], ((0, LANES - GLA_RANK), (0, 0))).astype(BF16)
            o = _gla_core(proj, a, wg, gla_b_gate[j][None], gla_norm[j][None], bsz=bsz, seq=seq)
        else:
            n_main = fox_w_in.shape[2] - FOX_HEADS
            proj, fl, (w_up16, w_down16, w_out16) = _norm_proj(
                xf, norm_mix[i][None], sc_m, sh_m, fox_w_in16[j], _pad_cols(fox_w_in16[j, :, n_main:], LANES),
                [(ffn_w_up, i), (ffn_w_down, i), (fox_w_out, j)], n=n_main, seq=seq)
            ck = _fox_cum(fl, _pad_cols(fox_b_f[j][None], LANES), bsz=bsz, seq=seq)
            o = _fox_attention(proj, ck.reshape(bsz, FOX_HEADS, 1, seq), fox_q_norm[j][None],
                               fox_k_norm[j][None], bsz=bsz, seq=seq)
        xf = _proj_residual(o, w_out16, xf, g_m, seq=seq)
        xf = _conv_ffn(xf, norm_ffn[i][None], sc_f, sh_f, g_f, w_up16, ffn_conv_w, conv_b, w_down16,
                       norm_final[None] if i == depth - 1 else None, layer=i, seq=seq)
    return xf.reshape(bsz, seq, d)
```

```python
import functools

import jax
import jax.numpy as jnp
from jax import lax
from jax.experimental import pallas as pl
from jax.experimental.pallas import tpu as pltpu

F32 = jnp.float32
BF16 = jnp.bfloat16

NORM_EPS = 1e-6
GLA_HEADS = 4
GLA_RANK = 16
GLA_TAU = 16.0
GLA_CHUNK = 64
FOX_HEADS = 16
CONV_WIDTH = 3
FFN_ROW_CHUNK = 32
FFN_DOT_ROWS = 512
NORM_ROW_CHUNK = 32

LANES = 128
SUBLANES = 8
MXU_COLS = 256
V7X_VMEM_BYTES = 64 * 1024 * 1024
VMEM_LIMIT_BYTES = V7X_VMEM_BYTES - 8 * 1024 * 1024

LOG2_E = 1.4426950408889634

NT_DIMS = (((1,), (1,)), ((), ()))
TN_DIMS = (((0,), (0,)), ((), ()))


def _params(*semantics):
    return pltpu.CompilerParams(dimension_semantics=semantics, vmem_limit_bytes=VMEM_LIMIT_BYTES)


def _dot(a, b):
    return jnp.dot(a, b, preferred_element_type=F32)


def _rms_scale(x):
    return x * lax.rsqrt(jnp.mean(x * x, axis=-1, keepdims=True) + NORM_EPS)


def _log_sigmoid(x):
    return jnp.minimum(x, 0.0) - jnp.log1p(jnp.exp(-jnp.abs(x)))


def _split_bf16(x, parts):
    out = []
    for _ in range(parts):
        hi = x.astype(BF16)
        out.append(hi)
        x = x - hi.astype(F32)
    return out


def _mod_kernel(c_ref, w_ref, b_ref, o_ref):
    c = c_ref[...]
    cond = c * jax.nn.sigmoid(c)
    o_ref[0] = _dot(cond.astype(BF16), w_ref[0].astype(BF16)) + b_ref[0]


def _modulation(c, w_mod, b_mod, *, tn=1024):
    depth, d, n = w_mod.shape
    bsz = c.shape[0]
    return pl.pallas_call(
        _mod_kernel,
        grid=(depth, n // tn),
        in_specs=[
            pl.BlockSpec((bsz, d), lambda l, j: (0, 0)),
            pl.BlockSpec((1, d, tn), lambda l, j: (l, 0, j)),
            pl.BlockSpec((1, 1, tn), lambda l, j: (l, 0, j)),
        ],
        out_specs=pl.BlockSpec((1, bsz, tn), lambda l, j: (l, 0, j)),
        out_shape=jax.ShapeDtypeStruct((depth, bsz, n), F32),
        compiler_params=_params("parallel", "parallel"),
        name="adaln_modulation",
    )(c, w_mod, b_mod.reshape(depth, 1, n))


def _norm_proj_kernel(x_ref, gain_ref, sc_ref, sh_ref, w_ref, ws_ref, *rest, n_cast):
    cast_src = rest[:n_cast]
    o_ref, os_ref = rest[n_cast:n_cast + 2]
    cast_dst = rest[n_cast + 2:2 * n_cast + 2]
    h_ref = rest[-1]

    @pl.when(pl.program_id(1) == 0)
    def _():
        h = _rms_scale(x_ref[...]) * gain_ref[...] * (1.0 + sc_ref[0]) + sh_ref[0]
        h_ref[...] = h.astype(BF16)
        os_ref[...] = _dot(h_ref[...], ws_ref[...])

    o_ref[...] = _dot(h_ref[...], w_ref[...]).astype(o_ref.dtype)

    for src, dst in zip(cast_src, cast_dst):
        dst[...] = src[...].astype(BF16)


def _cast_block_rows(rows, steps):
    bf16_rows = 2 * SUBLANES
    for r in range(bf16_rows, rows + 1, bf16_rows):
        if rows % r == 0 and rows // r <= steps:
            return r
    raise ValueError("weight does not fit the side cast")


def _cast_specs(casts, steps, flat_step):
    specs_in, specs_out, shapes = [], [], []
    for stacked, layer in casts:
        _, rows, cols = stacked.shape
        r = _cast_block_rows(rows, steps)
        last = rows // r - 1
        specs_in.append(pl.BlockSpec(
            (None, r, cols), lambda *idx, layer=layer, last=last: (layer, jnp.minimum(flat_step(*idx), last), 0)))
        specs_out.append(pl.BlockSpec(
            (r, cols), lambda *idx, last=last: (jnp.minimum(flat_step(*idx), last), 0)))
        shapes.append(jax.ShapeDtypeStruct((rows, cols), BF16))
    return specs_in, specs_out, shapes


def _norm_proj(x, gain, sc, sh, w, w_small, casts, *, n, seq, tm=1024, tn=1024):
    m, d = x.shape
    assert n % tn == 0 and n <= w.shape[1]
    tiles_per_seq = seq // tm
    ni, nj = m // tm, n // tn
    cast_specs_in, cast_specs_out, cast_shapes = _cast_specs(casts, ni * nj, lambda i, j: i * nj + j)
    outs = pl.pallas_call(
        functools.partial(_norm_proj_kernel, n_cast=len(casts)),
        grid=(ni, nj),
        in_specs=[
            pl.BlockSpec((tm, d), lambda i, j: (i, 0)),
            pl.BlockSpec((1, d), lambda i, j: (0, 0)),
            pl.BlockSpec((1, 1, d), lambda i, j: (i // tiles_per_seq, 0, 0)),
            pl.BlockSpec((1, 1, d), lambda i, j: (i // tiles_per_seq, 0, 0)),
            pl.BlockSpec((d, tn), lambda i, j: (0, j)),
            pl.BlockSpec((d, LANES), lambda i, j: (0, 0)),
        ] + cast_specs_in,
        out_specs=[
            pl.BlockSpec((tm, tn), lambda i, j: (i, j)),
            pl.BlockSpec((tm, LANES), lambda i, j: (i, 0)),
        ] + cast_specs_out,
        out_shape=[
            jax.ShapeDtypeStruct((m, n), BF16),
            jax.ShapeDtypeStruct((m, LANES), F32),
        ] + cast_shapes,
        scratch_shapes=[pltpu.VMEM((tm, d), BF16)],
        compiler_params=_params("arbitrary", "arbitrary"),
        name="norm_modulate_in_proj",
    )(x, gain, sc, sh, w, w_small, *[stacked for stacked, _ in casts])
    return outs[0], outs[1], outs[2:]


def _proj_residual_kernel(a_ref, w_ref, x_ref, g_ref, o_ref):
    o_ref[...] = x_ref[...] + (1.0 + g_ref[0]) * _dot(a_ref[...], w_ref[...])


def _proj_residual(a, w, x, g, *, seq, tm=1024):
    m, k = a.shape
    n = w.shape[1]
    tiles_per_seq = seq // tm
    return pl.pallas_call(
        _proj_residual_kernel,
        grid=(m // tm,),
        in_specs=[
            pl.BlockSpec((tm, k), lambda i: (i, 0)),
            pl.BlockSpec((k, n), lambda i: (0, 0), pipeline_mode=pl.Buffered(1)),
            pl.BlockSpec((tm, n), lambda i: (i, 0)),
            pl.BlockSpec((1, 1, n), lambda i: (i // tiles_per_seq, 0, 0)),
        ],
        out_specs=pl.BlockSpec((tm, n), lambda i: (i, 0)),
        out_shape=jax.ShapeDtypeStruct((m, n), F32),
        compiler_params=_params("parallel"),
        name="out_proj_residual",
    )(a, w, x, g)


def _gla_kernel(q_ref, k_ref, v_ref, r_ref, a_ref, wg_ref, bg_ref, gn_ref, *rest, chunk, heads, n_cast):
    cast_src = rest[:n_cast]
    o_ref = rest[n_cast]
    cast_dst = rest[n_cast + 1:2 * n_cast + 1]
    state_ref = rest[-1]

    @pl.when(pl.program_id(2) == 0)
    def _():
        state_ref[...] = jnp.zeros_like(state_ref)

    for src, dst in zip(cast_src, cast_dst):
        dst[...] = src[...].astype(BF16)

    rows = q_ref.shape[0]
    dk = q_ref.shape[1] // heads
    dv = v_ref.shape[1] // heads
    q_scale = dk ** -0.5
    log_alpha = _log_sigmoid(_dot(a_ref[...].astype(BF16), wg_ref[...]) + bg_ref[...]) / GLA_TAU

    row = lax.broadcasted_iota(jnp.int32, (chunk, chunk), 0)
    col = lax.broadcasted_iota(jnp.int32, (chunk, chunk), 1)
    causal = row >= col
    tri = causal.astype(BF16)

    for c in range(rows // chunk):
        sl = pl.ds(c * chunk, chunk)
        for g in range(heads):
            kc = slice(g * dk, (g + 1) * dk)
            vc = slice(g * dv, (g + 1) * dv)
            la = log_alpha[c * chunk:(c + 1) * chunk, kc]
            b = sum(_dot(tri, part) for part in _split_bf16(la, 2))
            b_last = b[chunk - 1:chunk]
            q = q_ref[sl, kc].astype(F32)
            k = k_ref[sl, kc].astype(F32)
            v = v_ref[sl, vc]
            q_dec = (q * (q_scale * jnp.exp(b))).astype(BF16)
            k_inv = (k * jnp.exp(-b)).astype(BF16)
            k_end = (k * jnp.exp(b_last - b)).astype(BF16)

            attn = jnp.where(causal, lax.dot_general(q_dec, k_inv, NT_DIMS, preferred_element_type=F32), 0.0)
            state = state_ref[g]
            o = _dot(attn.astype(BF16), v)
            o = o + lax.dot_general(q_dec, state.astype(BF16), NT_DIMS, preferred_element_type=F32)
            state_ref[g] = state * jnp.exp(b_last) + lax.dot_general(v, k_end, TN_DIMS, preferred_element_type=F32)

            r = r_ref[sl, vc].astype(F32)
            o = (_rms_scale(o) * gn_ref[:, vc]) * (r * jax.nn.sigmoid(r))
            o_ref[sl, vc] = o.astype(o_ref.dtype)


def _gla_core(proj, a, wg, bg, gn, casts, *, bsz, seq, rows=512, heads=2):
    m = proj.shape[0]
    dk = wg.shape[1] // GLA_HEADS
    dv = gn.shape[1] // GLA_HEADS
    nblk = seq // rows
    groups = GLA_HEADS // heads
    wk, wv = heads * dk, heads * dv
    k_off = (GLA_HEADS * dk) // wk
    v_off = (2 * GLA_HEADS * dk) // wv
    r_off = v_off + groups

    def row_blk(b, h, t):
        return b * nblk + t

    cast_specs_in, cast_specs_out, cast_shapes = _cast_specs(
        casts, bsz * groups * nblk, lambda b, h, t: (b * groups + h) * nblk + t)
    outs = pl.pallas_call(
        functools.partial(_gla_kernel, chunk=GLA_CHUNK, heads=heads, n_cast=len(casts)),
        grid=(bsz, groups, nblk),
        in_specs=[
            pl.BlockSpec((rows, wk), lambda b, h, t: (row_blk(b, h, t), h)),
            pl.BlockSpec((rows, wk), lambda b, h, t: (row_blk(b, h, t), k_off + h)),
            pl.BlockSpec((rows, wv), lambda b, h, t: (row_blk(b, h, t), v_off + h)),
            pl.BlockSpec((rows, wv), lambda b, h, t: (row_blk(b, h, t), r_off + h)),
            pl.BlockSpec((rows, LANES), lambda b, h, t: (row_blk(b, h, t), 0)),
            pl.BlockSpec((LANES, wk), lambda b, h, t: (0, h)),
            pl.BlockSpec((1, wk), lambda b, h, t: (0, h)),
            pl.BlockSpec((1, wv), lambda b, h, t: (0, h)),
        ] + cast_specs_in,
        out_specs=[pl.BlockSpec((rows, wv), lambda b, h, t: (row_blk(b, h, t), h))] + cast_specs_out,
        out_shape=[jax.ShapeDtypeStruct((m, GLA_HEADS * dv), BF16)] + cast_shapes,
        scratch_shapes=[pltpu.VMEM((heads, dv, dk), F32)],
        compiler_params=_params("arbitrary", "arbitrary", "arbitrary"),
        name="gla_chunked",
    )(proj, proj, proj, proj, a, wg, bg, gn, *[stacked for stacked, _ in casts])
    return outs[0], outs[1:]


def _fox_cum_kernel(fl_ref, bf_ref, o_ref, *, blk):
    seq = fl_ref.shape[0]
    log_f = _log_sigmoid(fl_ref[...] + bf_ref[...])
    row = lax.broadcasted_iota(jnp.int32, (blk, blk), 0)
    col = lax.broadcasted_iota(jnp.int32, (blk, blk), 1)
    tri = (row >= col).astype(BF16)
    carry = jnp.zeros((1, LANES), F32)
    pieces = []
    for s in range(seq // blk):
        lf = log_f[s * blk:(s + 1) * blk]
        cum = sum(_dot(tri, part) for part in _split_bf16(lf, 3)) + carry
        carry = cum[blk - 1:blk]
        pieces.append(cum)
    cum = jnp.concatenate(pieces, axis=0)
    o_ref[0] = cum.T[:o_ref.shape[1]]


def _fox_cum(fl, bf, *, bsz, seq, blk=256):
    return pl.pallas_call(
        functools.partial(_fox_cum_kernel, blk=blk),
        grid=(bsz,),
        in_specs=[
            pl.BlockSpec((seq, LANES), lambda b: (b, 0)),
            pl.BlockSpec((1, LANES), lambda b: (0, 0)),
        ],
        out_specs=pl.BlockSpec((1, FOX_HEADS, seq), lambda b: (b, 0, 0)),
        out_shape=jax.ShapeDtypeStruct((bsz, FOX_HEADS, seq), F32),
        compiler_params=_params("parallel"),
        name="fox_cum_log_forget",
    )(fl, bf)


def _fox_attn_kernel(q_ref, k_ref, v_ref, og_ref, ck_ref, qg_ref, kg_ref, o_ref, kn_ref, qn_ref, m_ref, l_ref,
                     acc_ref, *, heads, hd, rs, tk):
    qi = pl.program_id(2)
    tq = q_ref.shape[0]

    @pl.when(qi == 0)
    def _():
        for g in range(heads):
            cols = slice(g * hd, (g + 1) * hd)
            kn_ref[:, cols] = (_rms_scale(k_ref[:, cols].astype(F32)) * kg_ref[...]).astype(BF16)

    m_ref[...] = jnp.full(m_ref.shape, -jnp.inf, F32)
    l_ref[...] = jnp.zeros(l_ref.shape, F32)
    acc_ref[...] = jnp.zeros(acc_ref.shape, F32)

    q_scale = LOG2_E * hd ** -0.5
    for g in range(heads):
        cols = slice(g * hd, (g + 1) * hd)
        qn_ref[:, cols] = ((_rms_scale(q_ref[:, cols].astype(F32)) * qg_ref[...]) * q_scale).astype(BF16)

    def scores(unit):
        g, r, off, _ = unit
        cols = slice(g * hd, (g + 1) * hd)
        kb = kn_ref[pl.ds(off, tk), cols]
        ckb = ck_ref[0, g, :, pl.ds(off, tk)] * LOG2_E
        return lax.dot_general(qn_ref[r * rs:(r + 1) * rs, cols], kb, NT_DIMS, preferred_element_type=F32) - ckb

    def softmax_update(unit, s):
        g, r, _, diag_shift = unit
        rows = slice(r * rs, (r + 1) * rs)
        if diag_shift is not None:
            visible = (lax.broadcasted_iota(jnp.int32, (rs, tk), 1) + diag_shift
                       <= lax.broadcasted_iota(jnp.int32, (rs, tk), 0))
            s = jnp.where(visible, s, -jnp.inf)
        m_prev = m_ref[g, rows]
        m_new = jnp.maximum(m_prev, jnp.max(s, axis=-1, keepdims=True))
        alpha = jnp.exp2(m_prev - m_new)
        p = jnp.exp2(s - jnp.concatenate([m_new] * (tk // LANES), axis=1))
        l_ref[g, rows] = alpha * l_ref[g, rows] + jnp.sum(p, axis=-1, keepdims=True)
        m_ref[g, rows] = m_new
        return p.astype(BF16), alpha

    def weighted_values(unit, p, alpha):
        g, r, off, _ = unit
        rows = slice(r * rs, (r + 1) * rs)
        vb = v_ref[pl.ds(off, tk), g * hd:(g + 1) * hd]
        acc_ref[g, rows] = alpha * acc_ref[g, rows] + _dot(p, vb)

    def run(units, lookahead=1):
        s = {u: scores(units[u]) for u in range(min(lookahead, len(units)))}
        for u, unit in enumerate(units):
            p, alpha = softmax_update(unit, s.pop(u))
            if u + lookahead < len(units):
                s[u + lookahead] = scores(units[u + lookahead])
            weighted_values(unit, p, alpha)

    def body(j, carry):
        off = pl.multiple_of(j * tk, tk)
        run([(g, r, off, None) for g in range(heads) for r in range(tq // rs)])
        return carry

    lax.fori_loop(0, qi * (tq // tk), body, 0)

    diag_units = []
    for c in range(tq // tk):
        for g in range(heads):
            for r in range(tq // rs):
                first_key, first_row = c * tk, r * rs
                if first_key > first_row + rs - 1:
                    continue
                unmasked = first_key + tk - 1 <= first_row
                diag_units.append((g, r, pl.multiple_of(qi * tq + first_key, tk),
                                   None if unmasked else first_key - first_row))
    run(diag_units)

    for g in range(heads):
        cols = slice(g * hd, (g + 1) * hd)
        og = og_ref[:, cols].astype(F32)
        o_ref[:, cols] = ((acc_ref[g] / l_ref[g]) * jax.nn.sigmoid(og)).astype(o_ref.dtype)


def _fox_attention(proj, ck, qg, kg, *, bsz, seq, tq=512, heads=4, rs=128, tk=256):
    m = proj.shape[0]
    hd = qg.shape[1]
    assert hd == LANES, "softmax statistics are kept lane-replicated at the head width"
    nq = seq // tq
    groups = FOX_HEADS // heads
    w = heads * hd
    return pl.pallas_call(
        functools.partial(_fox_attn_kernel, heads=heads, hd=hd, rs=rs, tk=tk),
        grid=(bsz, groups, nq),
        in_specs=[
            pl.BlockSpec((tq, w), lambda b, h, i: (b * nq + i, h)),
            pl.BlockSpec((seq, w), lambda b, h, i: (b, groups + h)),
            pl.BlockSpec((seq, w), lambda b, h, i: (b, 2 * groups + h)),
            pl.BlockSpec((tq, w), lambda b, h, i: (b * nq + i, 3 * groups + h)),
            pl.BlockSpec((1, heads, 1, seq), lambda b, h, i: (b, h, 0, 0)),
            pl.BlockSpec((1, hd), lambda b, h, i: (0, 0)),
            pl.BlockSpec((1, hd), lambda b, h, i: (0, 0)),
        ],
        out_specs=pl.BlockSpec((tq, w), lambda b, h, i: (b * nq + i, h)),
        out_shape=jax.ShapeDtypeStruct((m, FOX_HEADS * hd), BF16),
        scratch_shapes=[
            pltpu.VMEM((seq, w), BF16),
            pltpu.VMEM((tq, w), BF16),
            pltpu.VMEM((heads, tq, LANES), F32),
            pltpu.VMEM((heads, tq, LANES), F32),
            pltpu.VMEM((heads, tq, hd), F32),
        ],
        compiler_params=_params("parallel", "parallel", "arbitrary"),
        name="fox_attention",
    )(proj, proj, proj, proj, ck, qg, kg)


def _ffn_kernel(x_ref, gain_ref, sc_ref, sh_ref, g_ref, wg_ref, wv_ref, cwg_ref, cwv_ref, cbg_ref, cbv_ref,
                wd_ref, *rest, tiles_per_seq, final_norm):
    if final_norm:
        fin_ref, o_ref, h_ref, carry_ref, u_ref, act_ref = rest
    else:
        o_ref, h_ref, carry_ref, u_ref, act_ref = rest
    i = pl.program_id(0)
    j = pl.program_id(1)
    tm = x_ref.shape[0]

    @pl.when(j == 0)
    def _():
        o_ref[...] = jnp.zeros_like(o_ref)
        for r0 in range(0, tm, NORM_ROW_CHUNK):
            rows = slice(r0, r0 + NORM_ROW_CHUNK)
            h = _rms_scale(x_ref[rows]) * gain_ref[...] * (1.0 + sc_ref[0]) + sh_ref[0]
            h_ref[rows] = h.astype(BF16)

    @pl.when(i % tiles_per_seq == 0)
    def _():
        carry_ref[j] = jnp.zeros(carry_ref.shape[1:], F32)

    h = h_ref[...]

    def conv(k, r0, cols, cw_ref, cb_ref):
        out = cb_ref[:, cols]
        for s in range(CONV_WIDTH):
            rows = slice(SUBLANES - s + r0, SUBLANES - s + r0 + FFN_ROW_CHUNK)
            out = out + u_ref[k, rows, cols] * cw_ref[CONV_WIDTH - 1 - s:CONV_WIDTH - s, cols]
        return out

    col_blocks = [slice(c, c + MXU_COLS) for c in range(0, wg_ref.shape[1], MXU_COLS)]
    for cols in col_blocks:
        for k, w_ref in enumerate((wg_ref, wv_ref)):
            u_ref[k, :SUBLANES, cols] = carry_ref[j, k, :, cols]
            for m0 in range(0, tm, FFN_DOT_ROWS):
                u_ref[k, SUBLANES + m0:SUBLANES + m0 + FFN_DOT_ROWS, cols] = _dot(
                    h_ref[m0:m0 + FFN_DOT_ROWS], w_ref[:, cols])
            carry_ref[j, k, :, cols] = u_ref[k, tm:, cols]
        for r0 in range(0, tm, FFN_ROW_CHUNK):
            gate = conv(0, r0, cols, cwg_ref, cbg_ref)
            val = conv(1, r0, cols, cwv_ref, cbv_ref)
            act_ref[r0:r0 + FFN_ROW_CHUNK, cols] = ((gate * jax.nn.sigmoid(gate)) * val).astype(BF16)
    o_ref[...] += _dot(act_ref[...], wd_ref[...])

    @pl.when(j == pl.num_programs(1) - 1)
    def _():
        for r0 in range(0, tm, NORM_ROW_CHUNK):
            rows = slice(r0, r0 + NORM_ROW_CHUNK)
            y = x_ref[rows] + (1.0 + g_ref[0]) * o_ref[rows]
            if final_norm:
                y = _rms_scale(y) * fin_ref[...]
            o_ref[rows] = y


def _conv_ffn(x, gain, sc, sh, g, w_up, conv_w, conv_b, w_down, final_gain, *, layer, seq, tm=1024, tf=512):
    m, d = x.shape
    d_ff = w_down.shape[0]
    nj = d_ff // tf
    tiles_per_seq = seq // tm
    final_norm = final_gain is not None

    def per_seq(i, j):
        return (i // tiles_per_seq, 0, 0)

    in_specs = [
        pl.BlockSpec((tm, d), lambda i, j: (i, 0), pipeline_mode=pl.Buffered(1)),
        pl.BlockSpec((1, d), lambda i, j: (0, 0)),
        pl.BlockSpec((1, 1, d), per_seq),
        pl.BlockSpec((1, 1, d), per_seq),
        pl.BlockSpec((1, 1, d), per_seq),
        pl.BlockSpec((d, tf), lambda i, j: (0, j)),
        pl.BlockSpec((d, tf), lambda i, j: (0, nj + j)),
        pl.BlockSpec((None, CONV_WIDTH, tf), lambda i, j: (layer, 0, j)),
        pl.BlockSpec((None, CONV_WIDTH, tf), lambda i, j: (layer, 0, nj + j)),
        pl.BlockSpec((None, 1, tf), lambda i, j: (layer, 0, j)),
        pl.BlockSpec((None, 1, tf), lambda i, j: (layer, 0, nj + j)),
        pl.BlockSpec((tf, d), lambda i, j: (j, 0)),
    ]
    args = [x, gain, sc, sh, g, w_up, w_up, conv_w, conv_w, conv_b, conv_b, w_down]
    if final_norm:
        in_specs.append(pl.BlockSpec((1, d), lambda i, j: (0, 0)))
        args.append(final_gain)
    return pl.pallas_call(
        functools.partial(_ffn_kernel, tiles_per_seq=tiles_per_seq, final_norm=final_norm),
        grid=(m // tm, nj),
        in_specs=in_specs,
        out_specs=pl.BlockSpec((tm, d), lambda i, j: (i, 0), pipeline_mode=pl.Buffered(1)),
        out_shape=jax.ShapeDtypeStruct((m, d), F32),
        scratch_shapes=[
            pltpu.VMEM((tm, d), BF16),
            pltpu.VMEM((nj, 2, SUBLANES, tf), F32),
            pltpu.VMEM((2, SUBLANES + tm, tf), F32),
            pltpu.VMEM((tm, tf), BF16),
        ],
        compiler_params=_params("arbitrary", "arbitrary"),
        name="conv_ffn",
    )(*args)


def _pad_cols(w, width):
    return jnp.pad(w, ((0, 0), (0, width - w.shape[1])))


def kernel(x, c, w_mod, b_mod, norm_mix, norm_ffn, gla_w_in, gla_w_gate, gla_b_gate, gla_norm, gla_w_out,
           fox_w_in, fox_b_f, fox_q_norm, fox_k_norm, fox_w_out, ffn_w_up, ffn_conv_w, ffn_conv_b,
           ffn_w_down, norm_final):
    bsz, seq, d = x.shape
    depth = w_mod.shape[0]
    m = bsz * seq
    xf = x.reshape(m, d)

    mod = _modulation(c, w_mod, b_mod).reshape(depth, bsz, 6, 1, d)

    gla_w_in16 = gla_w_in.astype(BF16)
    conv_b = ffn_conv_b[:, None, :]

    for i in range(depth):
        sh_m, sc_m, g_m, sh_f, sc_f, g_f = (mod[i, :, t] for t in range(6))
        j = i // 2
        if i % 2 == 0:
            n_main = gla_w_in.shape[2] - GLA_RANK
            proj, a, (w_up16, w_down16, w_out16) = _norm_proj(
                xf, norm_mix[i][None], sc_m, sh_m, gla_w_in16[j], _pad_cols(gla_w_in16[j, :, n_main:], LANES),
                [(ffn_w_up, i), (ffn_w_down, i), (gla_w_out, j)], n=n_main, seq=seq)
            wg = jnp.pad(gla_w_gate[j], ((0, LANES - GLA_RANK), (0, 0))).astype(BF16)
            o, (fox_w_in16,) = _gla_core(proj, a, wg, gla_b_gate[j][None], gla_norm[j][None], [(fox_w_in, j)],
                                         bsz=bsz, seq=seq)
        else:
            n_main = fox_w_in.shape[2] - FOX_HEADS
            proj, fl, (w_up16, w_down16, w_out16) = _norm_proj(
                xf, norm_mix[i][None], sc_m, sh_m, fox_w_in16, _pad_cols(fox_w_in16[:, n_main:], LANES),
                [(ffn_w_up, i), (ffn_w_down, i), (fox_w_out, j)], n=n_main, seq=seq)
            ck = _fox_cum(fl, _pad_cols(fox_b_f[j][None], LANES), bsz=bsz, seq=seq)
            o = _fox_attention(proj, ck.reshape(bsz, FOX_HEADS, 1, seq), fox_q_norm[j][None],
                               fox_k_norm[j][None], bsz=bsz, seq=seq)
        xf = _proj_residual(o, w_out16, xf, g_m, seq=seq)
        xf = _conv_ffn(xf, norm_ffn[i][None], sc_f, sh_f, g_f, w_up16, ffn_conv_w, conv_b, w_down16,
                       norm_final[None] if i == depth - 1 else None, layer=i, seq=seq)
    return xf.reshape(bsz, seq, d)
```

```python
import functools

import jax
import jax.numpy as jnp
from jax import lax
from jax.experimental import pallas as pl
from jax.experimental.pallas import tpu as pltpu

F32 = jnp.float32
BF16 = jnp.bfloat16

NORM_EPS = 1e-6
GLA_HEADS = 4
GLA_RANK = 16
GLA_TAU = 16.0
GLA_CHUNK = 64
FOX_HEADS = 16
CONV_WIDTH = 3
FFN_ROW_CHUNK = 32
FFN_DOT_ROWS = 512
NORM_ROW_CHUNK = 32

LANES = 128
SUBLANES = 8
MXU_COLS = 256
V7X_VMEM_BYTES = 64 * 1024 * 1024
VMEM_LIMIT_BYTES = V7X_VMEM_BYTES - 8 * 1024 * 1024

LOG2_E = 1.4426950408889634

NT_DIMS = (((1,), (1,)), ((), ()))
TN_DIMS = (((0,), (0,)), ((), ()))


def _params(*semantics):
    return pltpu.CompilerParams(dimension_semantics=semantics, vmem_limit_bytes=VMEM_LIMIT_BYTES)


def _dot(a, b):
    return jnp.dot(a, b, preferred_element_type=F32)


def _rms_scale(x):
    return x * lax.rsqrt(jnp.mean(x * x, axis=-1, keepdims=True) + NORM_EPS)


def _log_sigmoid(x):
    return jnp.minimum(x, 0.0) - jnp.log1p(jnp.exp(-jnp.abs(x)))


def _split_bf16(x, parts):
    out = []
    for _ in range(parts):
        hi = x.astype(BF16)
        out.append(hi)
        x = x - hi.astype(F32)
    return out


def _mod_kernel(c_ref, w_ref, b_ref, *rest, n_cast):
    cast_src, o_ref, cast_dst = rest[:n_cast], rest[n_cast], rest[n_cast + 1:]
    c = c_ref[...]
    cond = c * jax.nn.sigmoid(c)
    o_ref[0] = _dot(cond.astype(BF16), w_ref[0].astype(BF16)) + b_ref[0]
    for src, dst in zip(cast_src, cast_dst):
        dst[...] = src[...].astype(BF16)


def _modulation(c, w_mod, b_mod, casts, *, tn=1024):
    depth, d, n = w_mod.shape
    bsz = c.shape[0]
    nj = n // tn
    cast_specs_in, cast_specs_out, cast_shapes = _cast_specs(casts, depth * nj, lambda l, j: l * nj + j)
    outs = pl.pallas_call(
        functools.partial(_mod_kernel, n_cast=len(casts)),
        grid=(depth, nj),
        in_specs=[
            pl.BlockSpec((bsz, d), lambda l, j: (0, 0)),
            pl.BlockSpec((1, d, tn), lambda l, j: (l, 0, j)),
            pl.BlockSpec((1, 1, tn), lambda l, j: (l, 0, j)),
        ] + cast_specs_in,
        out_specs=[pl.BlockSpec((1, bsz, tn), lambda l, j: (l, 0, j))] + cast_specs_out,
        out_shape=[jax.ShapeDtypeStruct((depth, bsz, n), F32)] + cast_shapes,
        compiler_params=_params("arbitrary", "arbitrary"),
        name="adaln_modulation",
    )(c, w_mod, b_mod.reshape(depth, 1, n), *[stacked for stacked, _, _ in casts])
    return outs[0], outs[1:]


def _norm_proj_kernel(x_ref, gain_ref, sc_ref, sh_ref, w_ref, ws_ref, *rest, n_cast):
    cast_src = rest[:n_cast]
    o_ref, os_ref = rest[n_cast:n_cast + 2]
    cast_dst = rest[n_cast + 2:2 * n_cast + 2]
    h_ref = rest[-1]

    @pl.when(pl.program_id(1) == 0)
    def _():
        h = _rms_scale(x_ref[...]) * gain_ref[...] * (1.0 + sc_ref[0]) + sh_ref[0]
        h_ref[...] = h.astype(BF16)
        os_ref[...] = _dot(h_ref[...], ws_ref[...])

    o_ref[...] = _dot(h_ref[...], w_ref[...]).astype(o_ref.dtype)

    for src, dst in zip(cast_src, cast_dst):
        dst[...] = src[...].astype(BF16)


def _cast_block_rows(rows, steps):
    bf16_rows = 2 * SUBLANES
    for r in range(bf16_rows, rows + 1, bf16_rows):
        if rows % r == 0 and rows // r <= steps:
            return r
    raise ValueError("weight does not fit the side cast")


def _cast_specs(casts, steps, flat_step):
    specs_in, specs_out, shapes = [], [], []
    for stacked, layer, cols in casts:
        rows = stacked.shape[1]
        assert cols == stacked.shape[2] or cols % LANES == 0
        r = _cast_block_rows(rows, steps)
        last = rows // r - 1
        specs_in.append(pl.BlockSpec(
            (None, r, cols), lambda *idx, layer=layer, last=last: (layer, jnp.minimum(flat_step(*idx), last), 0)))
        specs_out.append(pl.BlockSpec(
            (r, cols), lambda *idx, last=last: (jnp.minimum(flat_step(*idx), last), 0)))
        shapes.append(jax.ShapeDtypeStruct((rows, cols), BF16))
    return specs_in, specs_out, shapes


def _norm_proj(x, gain, sc, sh, w, w_small, casts, *, n, seq, tm=1024, tn=1024):
    m, d = x.shape
    assert n % tn == 0 and n <= w.shape[1]
    tiles_per_seq = seq // tm
    ni, nj = m // tm, n // tn
    cast_specs_in, cast_specs_out, cast_shapes = _cast_specs(casts, ni * nj, lambda i, j: i * nj + j)
    outs = pl.pallas_call(
        functools.partial(_norm_proj_kernel, n_cast=len(casts)),
        grid=(ni, nj),
        in_specs=[
            pl.BlockSpec((tm, d), lambda i, j: (i, 0)),
            pl.BlockSpec((1, d), lambda i, j: (0, 0)),
            pl.BlockSpec((1, 1, d), lambda i, j: (i // tiles_per_seq, 0, 0)),
            pl.BlockSpec((1, 1, d), lambda i, j: (i // tiles_per_seq, 0, 0)),
            pl.BlockSpec((d, tn), lambda i, j: (0, j)),
            pl.BlockSpec((d, LANES), lambda i, j: (0, 0)),
        ] + cast_specs_in,
        out_specs=[
            pl.BlockSpec((tm, tn), lambda i, j: (i, j)),
            pl.BlockSpec((tm, LANES), lambda i, j: (i, 0)),
        ] + cast_specs_out,
        out_shape=[
            jax.ShapeDtypeStruct((m, n), BF16),
            jax.ShapeDtypeStruct((m, LANES), F32),
        ] + cast_shapes,
        scratch_shapes=[pltpu.VMEM((tm, d), BF16)],
        compiler_params=_params("arbitrary", "arbitrary"),
        name="norm_modulate_in_proj",
    )(x, gain, sc, sh, w, w_small, *[stacked for stacked, _, _ in casts])
    return outs[0], outs[1], outs[2:]


def _proj_residual_kernel(a_ref, w_ref, x_ref, g_ref, o_ref):
    o_ref[...] = x_ref[...] + (1.0 + g_ref[0]) * _dot(a_ref[...], w_ref[...])


def _proj_residual(a, w, x, g, *, seq, tm=1024):
    m, k = a.shape
    n = w.shape[1]
    tiles_per_seq = seq // tm
    return pl.pallas_call(
        _proj_residual_kernel,
        grid=(m // tm,),
        in_specs=[
            pl.BlockSpec((tm, k), lambda i: (i, 0)),
            pl.BlockSpec((k, n), lambda i: (0, 0), pipeline_mode=pl.Buffered(1)),
            pl.BlockSpec((tm, n), lambda i: (i, 0)),
            pl.BlockSpec((1, 1, n), lambda i: (i // tiles_per_seq, 0, 0)),
        ],
        out_specs=pl.BlockSpec((tm, n), lambda i: (i, 0)),
        out_shape=jax.ShapeDtypeStruct((m, n), F32),
        compiler_params=_params("parallel"),
        name="out_proj_residual",
    )(a, w, x, g)


def _gla_kernel(q_ref, k_ref, v_ref, r_ref, a_ref, wg_ref, bg_ref, gn_ref, *rest, chunk, heads, n_cast):
    cast_src = rest[:n_cast]
    o_ref = rest[n_cast]
    cast_dst = rest[n_cast + 1:2 * n_cast + 1]
    state_ref = rest[-1]

    @pl.when(pl.program_id(2) == 0)
    def _():
        state_ref[...] = jnp.zeros_like(state_ref)

    for src, dst in zip(cast_src, cast_dst):
        dst[...] = src[...].astype(BF16)

    rows = q_ref.shape[0]
    dk = q_ref.shape[1] // heads
    dv = v_ref.shape[1] // heads
    q_scale = dk ** -0.5
    log_alpha = _log_sigmoid(_dot(a_ref[...].astype(BF16), wg_ref[...]) + bg_ref[...]) / GLA_TAU

    row = lax.broadcasted_iota(jnp.int32, (chunk, chunk), 0)
    col = lax.broadcasted_iota(jnp.int32, (chunk, chunk), 1)
    causal = row >= col
    tri = causal.astype(BF16)

    for c in range(rows // chunk):
        sl = pl.ds(c * chunk, chunk)
        for g in range(heads):
            kc = slice(g * dk, (g + 1) * dk)
            vc = slice(g * dv, (g + 1) * dv)
            la = log_alpha[c * chunk:(c + 1) * chunk, kc]
            b = sum(_dot(tri, part) for part in _split_bf16(la, 2))
            b_last = b[chunk - 1:chunk]
            q = q_ref[sl, kc].astype(F32)
            k = k_ref[sl, kc].astype(F32)
            v = v_ref[sl, vc]
            q_dec = (q * (q_scale * jnp.exp(b))).astype(BF16)
            k_inv = (k * jnp.exp(-b)).astype(BF16)
            k_end = (k * jnp.exp(b_last - b)).astype(BF16)

            attn = jnp.where(causal, lax.dot_general(q_dec, k_inv, NT_DIMS, preferred_element_type=F32), 0.0)
            state = state_ref[g]
            o = _dot(attn.astype(BF16), v)
            o = o + lax.dot_general(q_dec, state.astype(BF16), NT_DIMS, preferred_element_type=F32)
            state_ref[g] = state * jnp.exp(b_last) + lax.dot_general(v, k_end, TN_DIMS, preferred_element_type=F32)

            r = r_ref[sl, vc].astype(F32)
            o = (_rms_scale(o) * gn_ref[:, vc]) * (r * jax.nn.sigmoid(r))
            o_ref[sl, vc] = o.astype(o_ref.dtype)


def _gla_core(proj, a, wg, bg, gn, casts, *, bsz, seq, rows=512, heads=2):
    m = proj.shape[0]
    dk = wg.shape[1] // GLA_HEADS
    dv = gn.shape[1] // GLA_HEADS
    nblk = seq // rows
    groups = GLA_HEADS // heads
    wk, wv = heads * dk, heads * dv
    k_off = (GLA_HEADS * dk) // wk
    v_off = (2 * GLA_HEADS * dk) // wv
    r_off = v_off + groups

    def row_blk(b, h, t):
        return b * nblk + t

    cast_specs_in, cast_specs_out, cast_shapes = _cast_specs(
        casts, bsz * groups * nblk, lambda b, h, t: (b * groups + h) * nblk + t)
    outs = pl.pallas_call(
        functools.partial(_gla_kernel, chunk=GLA_CHUNK, heads=heads, n_cast=len(casts)),
        grid=(bsz, groups, nblk),
        in_specs=[
            pl.BlockSpec((rows, wk), lambda b, h, t: (row_blk(b, h, t), h)),
            pl.BlockSpec((rows, wk), lambda b, h, t: (row_blk(b, h, t), k_off + h)),
            pl.BlockSpec((rows, wv), lambda b, h, t: (row_blk(b, h, t), v_off + h)),
            pl.BlockSpec((rows, wv), lambda b, h, t: (row_blk(b, h, t), r_off + h)),
            pl.BlockSpec((rows, LANES), lambda b, h, t: (row_blk(b, h, t), 0)),
            pl.BlockSpec((LANES, wk), lambda b, h, t: (0, h)),
            pl.BlockSpec((1, wk), lambda b, h, t: (0, h)),
            pl.BlockSpec((1, wv), lambda b, h, t: (0, h)),
        ] + cast_specs_in,
        out_specs=[pl.BlockSpec((rows, wv), lambda b, h, t: (row_blk(b, h, t), h))] + cast_specs_out,
        out_shape=[jax.ShapeDtypeStruct((m, GLA_HEADS * dv), BF16)] + cast_shapes,
        scratch_shapes=[pltpu.VMEM((heads, dv, dk), F32)],
        compiler_params=_params("arbitrary", "arbitrary", "arbitrary"),
        name="gla_chunked",
    )(proj, proj, proj, proj, a, wg, bg, gn, *[stacked for stacked, _, _ in casts])
    return outs[0], outs[1:]


def _fox_cum_kernel(fl_ref, bf_ref, o_ref, *, blk):
    seq = fl_ref.shape[0]
    log_f = _log_sigmoid(fl_ref[...] + bf_ref[...])
    row = lax.broadcasted_iota(jnp.int32, (blk, blk), 0)
    col = lax.broadcasted_iota(jnp.int32, (blk, blk), 1)
    tri = (row >= col).astype(BF16)
    carry = jnp.zeros((1, LANES), F32)
    pieces = []
    for s in range(seq // blk):
        lf = log_f[s * blk:(s + 1) * blk]
        cum = sum(_dot(tri, part) for part in _split_bf16(lf, 3)) + carry
        carry = cum[blk - 1:blk]
        pieces.append(cum)
    cum = jnp.concatenate(pieces, axis=0)
    o_ref[0] = cum.T[:o_ref.shape[1]]


def _fox_cum(fl, bf, *, bsz, seq, blk=256):
    return pl.pallas_call(
        functools.partial(_fox_cum_kernel, blk=blk),
        grid=(bsz,),
        in_specs=[
            pl.BlockSpec((seq, LANES), lambda b: (b, 0)),
            pl.BlockSpec((1, LANES), lambda b: (0, 0)),
        ],
        out_specs=pl.BlockSpec((1, FOX_HEADS, seq), lambda b: (b, 0, 0)),
        out_shape=jax.ShapeDtypeStruct((bsz, FOX_HEADS, seq), F32),
        compiler_params=_params("parallel"),
        name="fox_cum_log_forget",
    )(fl, bf)


def _fox_attn_kernel(q_ref, k_ref, v_ref, og_ref, ck_ref, qg_ref, kg_ref, o_ref, kn_ref, qn_ref, m_ref, l_ref,
                     acc_ref, *, heads, hd, rs, tk):
    qi = pl.program_id(2)
    tq = q_ref.shape[0]

    @pl.when(qi == 0)
    def _():
        for g in range(heads):
            cols = slice(g * hd, (g + 1) * hd)
            kn_ref[:, cols] = (_rms_scale(k_ref[:, cols].astype(F32)) * kg_ref[...]).astype(BF16)

    m_ref[...] = jnp.full(m_ref.shape, -jnp.inf, F32)
    l_ref[...] = jnp.zeros(l_ref.shape, F32)
    acc_ref[...] = jnp.zeros(acc_ref.shape, F32)

    q_scale = LOG2_E * hd ** -0.5
    for g in range(heads):
        cols = slice(g * hd, (g + 1) * hd)
        qn_ref[:, cols] = ((_rms_scale(q_ref[:, cols].astype(F32)) * qg_ref[...]) * q_scale).astype(BF16)

    def scores(unit):
        g, r, off, _ = unit
        cols = slice(g * hd, (g + 1) * hd)
        kb = kn_ref[pl.ds(off, tk), cols]
        ckb = ck_ref[0, g, :, pl.ds(off, tk)] * LOG2_E
        return lax.dot_general(qn_ref[r * rs:(r + 1) * rs, cols], kb, NT_DIMS, preferred_element_type=F32) - ckb

    def softmax_update(unit, s):
        g, r, _, diag_shift = unit
        rows = slice(r * rs, (r + 1) * rs)
        if diag_shift is not None:
            visible = (lax.broadcasted_iota(jnp.int32, (rs, tk), 1) + diag_shift
                       <= lax.broadcasted_iota(jnp.int32, (rs, tk), 0))
            s = jnp.where(visible, s, -jnp.inf)
        m_prev = m_ref[g, rows]
        m_new = jnp.maximum(m_prev, jnp.max(s, axis=-1, keepdims=True))
        alpha = jnp.exp2(m_prev - m_new)
        p = jnp.exp2(s - jnp.concatenate([m_new] * (tk // LANES), axis=1))
        l_ref[g, rows] = alpha * l_ref[g, rows] + jnp.sum(p, axis=-1, keepdims=True)
        m_ref[g, rows] = m_new
        return p.astype(BF16), alpha

    def weighted_values(unit, p, alpha):
        g, r, off, _ = unit
        rows = slice(r * rs, (r + 1) * rs)
        vb = v_ref[pl.ds(off, tk), g * hd:(g + 1) * hd]
        acc_ref[g, rows] = alpha * acc_ref[g, rows] + _dot(p, vb)

    def run(units, lookahead=1):
        s = {u: scores(units[u]) for u in range(min(lookahead, len(units)))}
        for u, unit in enumerate(units):
            p, alpha = softmax_update(unit, s.pop(u))
            if u + lookahead < len(units):
                s[u + lookahead] = scores(units[u + lookahead])
            weighted_values(unit, p, alpha)

    def body(j, carry):
        off = pl.multiple_of(j * tk, tk)
        run([(g, r, off, None) for g in range(heads) for r in range(tq // rs)])
        return carry

    lax.fori_loop(0, qi * (tq // tk), body, 0)

    diag_units = []
    for c in range(tq // tk):
        for g in range(heads):
            for r in range(tq // rs):
                first_key, first_row = c * tk, r * rs
                if first_key > first_row + rs - 1:
                    continue
                unmasked = first_key + tk - 1 <= first_row
                diag_units.append((g, r, pl.multiple_of(qi * tq + first_key, tk),
                                   None if unmasked else first_key - first_row))
    run(diag_units)

    for g in range(heads):
        cols = slice(g * hd, (g + 1) * hd)
        og = og_ref[:, cols].astype(F32)
        o_ref[:, cols] = ((acc_ref[g] / l_ref[g]) * jax.nn.sigmoid(og)).astype(o_ref.dtype)


def _fox_attention(proj, ck, qg, kg, *, bsz, seq, tq=512, heads=4, rs=128, tk=256):
    m = proj.shape[0]
    hd = qg.shape[1]
    assert hd == LANES, "softmax statistics are kept lane-replicated at the head width"
    nq = seq // tq
    groups = FOX_HEADS // heads
    w = heads * hd
    return pl.pallas_call(
        functools.partial(_fox_attn_kernel, heads=heads, hd=hd, rs=rs, tk=tk),
        grid=(bsz, groups, nq),
        in_specs=[
            pl.BlockSpec((tq, w), lambda b, h, i: (b * nq + i, h)),
            pl.BlockSpec((seq, w), lambda b, h, i: (b, groups + h)),
            pl.BlockSpec((seq, w), lambda b, h, i: (b, 2 * groups + h)),
            pl.BlockSpec((tq, w), lambda b, h, i: (b * nq + i, 3 * groups + h)),
            pl.BlockSpec((1, heads, 1, seq), lambda b, h, i: (b, h, 0, 0)),
            pl.BlockSpec((1, hd), lambda b, h, i: (0, 0)),
            pl.BlockSpec((1, hd), lambda b, h, i: (0, 0)),
        ],
        out_specs=pl.BlockSpec((tq, w), lambda b, h, i: (b * nq + i, h)),
        out_shape=jax.ShapeDtypeStruct((m, FOX_HEADS * hd), BF16),
        scratch_shapes=[
            pltpu.VMEM((seq, w), BF16),
            pltpu.VMEM((tq, w), BF16),
            pltpu.VMEM((heads, tq, LANES), F32),
            pltpu.VMEM((heads, tq, LANES), F32),
            pltpu.VMEM((heads, tq, hd), F32),
        ],
        compiler_params=_params("parallel", "parallel", "arbitrary"),
        name="fox_attention",
    )(proj, proj, proj, proj, ck, qg, kg)


def _ffn_kernel(x_ref, gain_ref, sc_ref, sh_ref, g_ref, wg_ref, wv_ref, cwg_ref, cwv_ref, cbg_ref, cbv_ref,
                wd_ref, *rest, tiles_per_seq, final_norm):
    if final_norm:
        fin_ref, o_ref, h_ref, carry_ref, u_ref, act_ref = rest
    else:
        o_ref, h_ref, carry_ref, u_ref, act_ref = rest
    i = pl.program_id(0)
    j = pl.program_id(1)
    tm = x_ref.shape[0]

    @pl.when(j == 0)
    def _():
        o_ref[...] = jnp.zeros_like(o_ref)
        for r0 in range(0, tm, NORM_ROW_CHUNK):
            rows = slice(r0, r0 + NORM_ROW_CHUNK)
            h = _rms_scale(x_ref[rows]) * gain_ref[...] * (1.0 + sc_ref[0]) + sh_ref[0]
            h_ref[rows] = h.astype(BF16)

    @pl.when(i % tiles_per_seq == 0)
    def _():
        carry_ref[j] = jnp.zeros(carry_ref.shape[1:], F32)

    h = h_ref[...]

    def conv(k, r0, cols, cw_ref, cb_ref):
        out = cb_ref[:, cols]
        for s in range(CONV_WIDTH):
            rows = slice(SUBLANES - s + r0, SUBLANES - s + r0 + FFN_ROW_CHUNK)
            out = out + u_ref[k, rows, cols] * cw_ref[CONV_WIDTH - 1 - s:CONV_WIDTH - s, cols]
        return out

    col_blocks = [slice(c, c + MXU_COLS) for c in range(0, wg_ref.shape[1], MXU_COLS)]
    for cols in col_blocks:
        for k, w_ref in enumerate((wg_ref, wv_ref)):
            u_ref[k, :SUBLANES, cols] = carry_ref[j, k, :, cols]
            for m0 in range(0, tm, FFN_DOT_ROWS):
                u_ref[k, SUBLANES + m0:SUBLANES + m0 + FFN_DOT_ROWS, cols] = _dot(
                    h_ref[m0:m0 + FFN_DOT_ROWS], w_ref[:, cols])
            carry_ref[j, k, :, cols] = u_ref[k, tm:, cols]
        for r0 in range(0, tm, FFN_ROW_CHUNK):
            gate = conv(0, r0, cols, cwg_ref, cbg_ref)
            val = conv(1, r0, cols, cwv_ref, cbv_ref)
            act_ref[r0:r0 + FFN_ROW_CHUNK, cols] = ((gate * jax.nn.sigmoid(gate)) * val).astype(BF16)
    o_ref[...] += _dot(act_ref[...], wd_ref[...])

    @pl.when(j == pl.num_programs(1) - 1)
    def _():
        for r0 in range(0, tm, NORM_ROW_CHUNK):
            rows = slice(r0, r0 + NORM_ROW_CHUNK)
            y = x_ref[rows] + (1.0 + g_ref[0]) * o_ref[rows]
            if final_norm:
                y = _rms_scale(y) * fin_ref[...]
            o_ref[rows] = y


def _conv_ffn(x, gain, sc, sh, g, w_up, conv_w, conv_b, w_down, final_gain, *, layer, seq, tm=1024, tf=512):
    m, d = x.shape
    d_ff = w_down.shape[0]
    nj = d_ff // tf
    tiles_per_seq = seq // tm
    final_norm = final_gain is not None

    def per_seq(i, j):
        return (i // tiles_per_seq, 0, 0)

    in_specs = [
        pl.BlockSpec((tm, d), lambda i, j: (i, 0), pipeline_mode=pl.Buffered(1)),
        pl.BlockSpec((1, d), lambda i, j: (0, 0)),
        pl.BlockSpec((1, 1, d), per_seq),
        pl.BlockSpec((1, 1, d), per_seq),
        pl.BlockSpec((1, 1, d), per_seq),
        pl.BlockSpec((d, tf), lambda i, j: (0, j)),
        pl.BlockSpec((d, tf), lambda i, j: (0, nj + j)),
        pl.BlockSpec((None, CONV_WIDTH, tf), lambda i, j: (layer, 0, j)),
        pl.BlockSpec((None, CONV_WIDTH, tf), lambda i, j: (layer, 0, nj + j)),
        pl.BlockSpec((None, 1, tf), lambda i, j: (layer, 0, j)),
        pl.BlockSpec((None, 1, tf), lambda i, j: (layer, 0, nj + j)),
        pl.BlockSpec((tf, d), lambda i, j: (j, 0)),
    ]
    args = [x, gain, sc, sh, g, w_up, w_up, conv_w, conv_w, conv_b, conv_b, w_down]
    if final_norm:
        in_specs.append(pl.BlockSpec((1, d), lambda i, j: (0, 0)))
        args.append(final_gain)
    return pl.pallas_call(
        functools.partial(_ffn_kernel, tiles_per_seq=tiles_per_seq, final_norm=final_norm),
        grid=(m // tm, nj),
        in_specs=in_specs,
        out_specs=pl.BlockSpec((tm, d), lambda i, j: (i, 0), pipeline_mode=pl.Buffered(1)),
        out_shape=jax.ShapeDtypeStruct((m, d), F32),
        scratch_shapes=[
            pltpu.VMEM((tm, d), BF16),
            pltpu.VMEM((nj, 2, SUBLANES, tf), F32),
            pltpu.VMEM((2, SUBLANES + tm, tf), F32),
            pltpu.VMEM((tm, tf), BF16),
        ],
        compiler_params=_params("arbitrary", "arbitrary"),
        name="conv_ffn",
    )(*args)


def _pad_cols(w, width):
    return jnp.pad(w, ((0, 0), (0, width - w.shape[1])))


def kernel(x, c, w_mod, b_mod, norm_mix, norm_ffn, gla_w_in, gla_w_gate, gla_b_gate, gla_norm, gla_w_out,
           fox_w_in, fox_b_f, fox_q_norm, fox_k_norm, fox_w_out, ffn_w_up, ffn_conv_w, ffn_conv_b,
           ffn_w_down, norm_final):
    bsz, seq, d = x.shape
    depth = w_mod.shape[0]
    m = bsz * seq
    xf = x.reshape(m, d)

    gla_main = gla_w_in.shape[2] - GLA_RANK
    fox_main = fox_w_in.shape[2] - FOX_HEADS
    mod, (gla_w_in16,) = _modulation(c, w_mod, b_mod, [(gla_w_in, 0, gla_main)])
    mod = mod.reshape(depth, bsz, 6, 1, d)
    conv_b = ffn_conv_b[:, None, :]

    def ffn_casts(layer):
        return [(ffn_w_up, layer, ffn_w_up.shape[2]), (ffn_w_down, layer, ffn_w_down.shape[2])]

    for i in range(depth):
        sh_m, sc_m, g_m, sh_f, sc_f, g_f = (mod[i, :, t] for t in range(6))
        j = i // 2
        if i % 2 == 0:
            assert j == 0, "the GLA in-proj cast rides on the single modulation call"
            proj, a, (w_up16, w_down16, w_out16) = _norm_proj(
                xf, norm_mix[i][None], sc_m, sh_m, gla_w_in16,
                _pad_cols(gla_w_in[j, :, gla_main:], LANES).astype(BF16),
                ffn_casts(i) + [(gla_w_out, j, gla_w_out.shape[2])], n=gla_main, seq=seq)
            wg = jnp.pad(gla_w_gate[j], ((0, LANES - GLA_RANK), (0, 0))).astype(BF16)
            o, (fox_w_in16,) = _gla_core(proj, a, wg, gla_b_gate[j][None], gla_norm[j][None],
                                         [(fox_w_in, j, fox_main)], bsz=bsz, seq=seq)
        else:
            proj, fl, (w_up16, w_down16, w_out16) = _norm_proj(
                xf, norm_mix[i][None], sc_m, sh_m, fox_w_in16,
                _pad_cols(fox_w_in[j, :, fox_main:], LANES).astype(BF16),
                ffn_casts(i) + [(fox_w_out, j, fox_w_out.shape[2])], n=fox_main, seq=seq)
            ck = _fox_cum(fl, _pad_cols(fox_b_f[j][None], LANES), bsz=bsz, seq=seq)
            o = _fox_attention(proj, ck.reshape(bsz, FOX_HEADS, 1, seq), fox_q_norm[j][None],
                               fox_k_norm[j][None], bsz=bsz, seq=seq)
        xf = _proj_residual(o, w_out16, xf, g_m, seq=seq)
        xf = _conv_ffn(xf, norm_ffn[i][None], sc_f, sh_f, g_f, w_up16, ffn_conv_w, conv_b, w_down16,
                       norm_final[None] if i == depth - 1 else None, layer=i, seq=seq)
    return xf.reshape(bsz, seq, d)
```

```python
import functools

import jax
import jax.numpy as jnp
from jax import lax
from jax.experimental import pallas as pl
from jax.experimental.pallas import tpu as pltpu

F32 = jnp.float32
BF16 = jnp.bfloat16

NORM_EPS = 1e-6
GLA_HEADS = 4
GLA_RANK = 16
GLA_TAU = 16.0
GLA_CHUNK = 64
FOX_HEADS = 16
CONV_WIDTH = 3
FFN_ROW_CHUNK = 32
NORM_ROW_CHUNK = 32

LANES = 128
SUBLANES = 8
MXU_COLS = 256
V7X_VMEM_BYTES = 64 * 1024 * 1024
VMEM_LIMIT_BYTES = V7X_VMEM_BYTES - 8 * 1024 * 1024

LOG2_E = 1.4426950408889634

NT_DIMS = (((1,), (1,)), ((), ()))
TN_DIMS = (((0,), (0,)), ((), ()))


def _params(*semantics):
    return pltpu.CompilerParams(dimension_semantics=semantics, vmem_limit_bytes=VMEM_LIMIT_BYTES)


def _dot(a, b):
    return jnp.dot(a, b, preferred_element_type=F32)


def _rms_scale(x):
    return x * lax.rsqrt(jnp.mean(x * x, axis=-1, keepdims=True) + NORM_EPS)


def _log_sigmoid(x):
    return jnp.minimum(x, 0.0) - jnp.log1p(jnp.exp(-jnp.abs(x)))


def _split_bf16(x, parts):
    out = []
    for _ in range(parts):
        hi = x.astype(BF16)
        out.append(hi)
        x = x - hi.astype(F32)
    return out


def _mod_kernel(c_ref, w_ref, b_ref, o_ref):
    c = c_ref[...]
    cond = c * jax.nn.sigmoid(c)
    o_ref[0] = _dot(cond.astype(BF16), w_ref[0].astype(BF16)) + b_ref[0]


def _modulation(c, w_mod, b_mod, *, tn=1024):
    depth, d, n = w_mod.shape
    bsz = c.shape[0]
    return pl.pallas_call(
        _mod_kernel,
        grid=(depth, n // tn),
        in_specs=[
            pl.BlockSpec((bsz, d), lambda l, j: (0, 0)),
            pl.BlockSpec((1, d, tn), lambda l, j: (l, 0, j)),
            pl.BlockSpec((1, 1, tn), lambda l, j: (l, 0, j)),
        ],
        out_specs=pl.BlockSpec((1, bsz, tn), lambda l, j: (l, 0, j)),
        out_shape=jax.ShapeDtypeStruct((depth, bsz, n), F32),
        compiler_params=_params("parallel", "parallel"),
        name="adaln_modulation",
    )(c, w_mod, b_mod.reshape(depth, 1, n))


def _norm_proj_kernel(x_ref, gain_ref, sc_ref, sh_ref, w_ref, ws_ref, *rest, n_cast):
    cast_src = rest[:n_cast]
    o_ref, os_ref = rest[n_cast:n_cast + 2]
    cast_dst = rest[n_cast + 2:2 * n_cast + 2]
    h_ref = rest[-1]

    @pl.when(pl.program_id(1) == 0)
    def _():
        h = _rms_scale(x_ref[...]) * gain_ref[...] * (1.0 + sc_ref[0]) + sh_ref[0]
        h_ref[...] = h.astype(BF16)
        os_ref[...] = _dot(h_ref[...], ws_ref[...])

    o_ref[...] = _dot(h_ref[...], w_ref[...]).astype(o_ref.dtype)

    for src, dst in zip(cast_src, cast_dst):
        dst[...] = src[...].astype(BF16)


def _cast_block_rows(rows, steps):
    bf16_rows = 2 * SUBLANES
    for r in range(bf16_rows, rows + 1, bf16_rows):
        if rows % r == 0 and rows // r <= steps:
            return r
    raise ValueError("weight does not fit the side cast")


def _cast_specs(casts, steps, flat_step):
    specs_in, specs_out, shapes = [], [], []
    for stacked, layer in casts:
        _, rows, cols = stacked.shape
        r = _cast_block_rows(rows, steps)
        last = rows // r - 1
        specs_in.append(pl.BlockSpec(
            (None, r, cols), lambda *idx, layer=layer, last=last: (layer, jnp.minimum(flat_step(*idx), last), 0)))
        specs_out.append(pl.BlockSpec(
            (r, cols), lambda *idx, last=last: (jnp.minimum(flat_step(*idx), last), 0)))
        shapes.append(jax.ShapeDtypeStruct((rows, cols), BF16))
    return specs_in, specs_out, shapes


def _norm_proj(x, gain, sc, sh, w, w_small, casts, *, n, seq, tm=1024, tn=1024):
    m, d = x.shape
    assert n % tn == 0 and n <= w.shape[1]
    tiles_per_seq = seq // tm
    ni, nj = m // tm, n // tn
    cast_specs_in, cast_specs_out, cast_shapes = _cast_specs(casts, ni * nj, lambda i, j: i * nj + j)
    outs = pl.pallas_call(
        functools.partial(_norm_proj_kernel, n_cast=len(casts)),
        grid=(ni, nj),
        in_specs=[
            pl.BlockSpec((tm, d), lambda i, j: (i, 0)),
            pl.BlockSpec((1, d), lambda i, j: (0, 0)),
            pl.BlockSpec((1, 1, d), lambda i, j: (i // tiles_per_seq, 0, 0)),
            pl.BlockSpec((1, 1, d), lambda i, j: (i // tiles_per_seq, 0, 0)),
            pl.BlockSpec((d, tn), lambda i, j: (0, j)),
            pl.BlockSpec((d, LANES), lambda i, j: (0, 0)),
        ] + cast_specs_in,
        out_specs=[
            pl.BlockSpec((tm, tn), lambda i, j: (i, j)),
            pl.BlockSpec((tm, LANES), lambda i, j: (i, 0)),
        ] + cast_specs_out,
        out_shape=[
            jax.ShapeDtypeStruct((m, n), BF16),
            jax.ShapeDtypeStruct((m, LANES), F32),
        ] + cast_shapes,
        scratch_shapes=[pltpu.VMEM((tm, d), BF16)],
        compiler_params=_params("arbitrary", "arbitrary"),
        name="norm_modulate_in_proj",
    )(x, gain, sc, sh, w, w_small, *[stacked for stacked, _ in casts])
    return outs[0], outs[1], outs[2:]


def _proj_residual_kernel(a_ref, w_ref, x_ref, g_ref, o_ref):
    o_ref[...] = x_ref[...] + (1.0 + g_ref[0]) * _dot(a_ref[...], w_ref[...])


def _proj_residual(a, w, x, g, *, seq, tm=1024):
    m, k = a.shape
    n = w.shape[1]
    tiles_per_seq = seq // tm
    return pl.pallas_call(
        _proj_residual_kernel,
        grid=(m // tm,),
        in_specs=[
            pl.BlockSpec((tm, k), lambda i: (i, 0)),
            pl.BlockSpec((k, n), lambda i: (0, 0), pipeline_mode=pl.Buffered(1)),
            pl.BlockSpec((tm, n), lambda i: (i, 0)),
            pl.BlockSpec((1, 1, n), lambda i: (i // tiles_per_seq, 0, 0)),
        ],
        out_specs=pl.BlockSpec((tm, n), lambda i: (i, 0)),
        out_shape=jax.ShapeDtypeStruct((m, n), F32),
        compiler_params=_params("parallel"),
        name="out_proj_residual",
    )(a, w, x, g)


def _gla_kernel(q_ref, k_ref, v_ref, r_ref, a_ref, wg_ref, bg_ref, gn_ref, o_ref, state_ref, *, chunk, heads):
    @pl.when(pl.program_id(2) == 0)
    def _():
        state_ref[...] = jnp.zeros_like(state_ref)

    rows = q_ref.shape[0]
    dk = q_ref.shape[1] // heads
    dv = v_ref.shape[1] // heads
    q_scale = dk ** -0.5
    log_alpha = _log_sigmoid(_dot(a_ref[...].astype(BF16), wg_ref[...]) + bg_ref[...]) / GLA_TAU

    row = lax.broadcasted_iota(jnp.int32, (chunk, chunk), 0)
    col = lax.broadcasted_iota(jnp.int32, (chunk, chunk), 1)
    causal = row >= col
    tri = causal.astype(BF16)

    for c in range(rows // chunk):
        sl = pl.ds(c * chunk, chunk)
        for g in range(heads):
            kc = slice(g * dk, (g + 1) * dk)
            vc = slice(g * dv, (g + 1) * dv)
            la = log_alpha[c * chunk:(c + 1) * chunk, kc]
            b = sum(_dot(tri, part) for part in _split_bf16(la, 2))
            b_last = b[chunk - 1:chunk]
            q = q_ref[sl, kc].astype(F32)
            k = k_ref[sl, kc].astype(F32)
            v = v_ref[sl, vc]
            q_dec = (q * (q_scale * jnp.exp(b))).astype(BF16)
            k_inv = (k * jnp.exp(-b)).astype(BF16)
            k_end = (k * jnp.exp(b_last - b)).astype(BF16)

            attn = jnp.where(causal, lax.dot_general(q_dec, k_inv, NT_DIMS, preferred_element_type=F32), 0.0)
            state = state_ref[g]
            o = _dot(attn.astype(BF16), v)
            o = o + lax.dot_general(q_dec, state.astype(BF16), NT_DIMS, preferred_element_type=F32)
            state_ref[g] = state * jnp.exp(b_last) + lax.dot_general(v, k_end, TN_DIMS, preferred_element_type=F32)

            r = r_ref[sl, vc].astype(F32)
            o = (_rms_scale(o) * gn_ref[:, vc]) * (r * jax.nn.sigmoid(r))
            o_ref[sl, vc] = o.astype(o_ref.dtype)


def _gla_core(proj, a, wg, bg, gn, *, bsz, seq, rows=512, heads=2):
    m = proj.shape[0]
    dk = wg.shape[1] // GLA_HEADS
    dv = gn.shape[1] // GLA_HEADS
    nblk = seq // rows
    groups = GLA_HEADS // heads
    wk, wv = heads * dk, heads * dv
    k_off = (GLA_HEADS * dk) // wk
    v_off = (2 * GLA_HEADS * dk) // wv
    r_off = v_off + groups

    def row_blk(b, h, t):
        return b * nblk + t

    return pl.pallas_call(
        functools.partial(_gla_kernel, chunk=GLA_CHUNK, heads=heads),
        grid=(bsz, groups, nblk),
        in_specs=[
            pl.BlockSpec((rows, wk), lambda b, h, t: (row_blk(b, h, t), h)),
            pl.BlockSpec((rows, wk), lambda b, h, t: (row_blk(b, h, t), k_off + h)),
            pl.BlockSpec((rows, wv), lambda b, h, t: (row_blk(b, h, t), v_off + h)),
            pl.BlockSpec((rows, wv), lambda b, h, t: (row_blk(b, h, t), r_off + h)),
            pl.BlockSpec((rows, LANES), lambda b, h, t: (row_blk(b, h, t), 0)),
            pl.BlockSpec((LANES, wk), lambda b, h, t: (0, h)),
            pl.BlockSpec((1, wk), lambda b, h, t: (0, h)),
            pl.BlockSpec((1, wv), lambda b, h, t: (0, h)),
        ],
        out_specs=pl.BlockSpec((rows, wv), lambda b, h, t: (row_blk(b, h, t), h)),
        out_shape=jax.ShapeDtypeStruct((m, GLA_HEADS * dv), BF16),
        scratch_shapes=[pltpu.VMEM((heads, dv, dk), F32)],
        compiler_params=_params("parallel", "parallel", "arbitrary"),
        name="gla_chunked",
    )(proj, proj, proj, proj, a, wg, bg, gn)


def _fox_cum_kernel(fl_ref, bf_ref, o_ref, *, blk):
    seq = fl_ref.shape[0]
    log_f = _log_sigmoid(fl_ref[...] + bf_ref[...])
    row = lax.broadcasted_iota(jnp.int32, (blk, blk), 0)
    col = lax.broadcasted_iota(jnp.int32, (blk, blk), 1)
    tri = (row >= col).astype(BF16)
    carry = jnp.zeros((1, LANES), F32)
    pieces = []
    for s in range(seq // blk):
        lf = log_f[s * blk:(s + 1) * blk]
        cum = sum(_dot(tri, part) for part in _split_bf16(lf, 3)) + carry
        carry = cum[blk - 1:blk]
        pieces.append(cum)
    cum = jnp.concatenate(pieces, axis=0)
    o_ref[0] = cum.T[:o_ref.shape[1]]


def _fox_cum(fl, bf, *, bsz, seq, blk=256):
    return pl.pallas_call(
        functools.partial(_fox_cum_kernel, blk=blk),
        grid=(bsz,),
        in_specs=[
            pl.BlockSpec((seq, LANES), lambda b: (b, 0)),
            pl.BlockSpec((1, LANES), lambda b: (0, 0)),
        ],
        out_specs=pl.BlockSpec((1, FOX_HEADS, seq), lambda b: (b, 0, 0)),
        out_shape=jax.ShapeDtypeStruct((bsz, FOX_HEADS, seq), F32),
        compiler_params=_params("parallel"),
        name="fox_cum_log_forget",
    )(fl, bf)


def _fox_attn_kernel(q_ref, k_ref, v_ref, og_ref, ck_ref, qg_ref, kg_ref, o_ref, kn_ref, qn_ref, m_ref, l_ref,
                     acc_ref, *, heads, hd, rs, tk):
    qi = pl.program_id(2)
    tq = q_ref.shape[0]

    @pl.when(qi == 0)
    def _():
        for g in range(heads):
            cols = slice(g * hd, (g + 1) * hd)
            kn_ref[:, cols] = (_rms_scale(k_ref[:, cols].astype(F32)) * kg_ref[...]).astype(BF16)

    m_ref[...] = jnp.full(m_ref.shape, -jnp.inf, F32)
    l_ref[...] = jnp.zeros(l_ref.shape, F32)
    acc_ref[...] = jnp.zeros(acc_ref.shape, F32)

    q_scale = LOG2_E * hd ** -0.5
    for g in range(heads):
        cols = slice(g * hd, (g + 1) * hd)
        qn_ref[:, cols] = ((_rms_scale(q_ref[:, cols].astype(F32)) * qg_ref[...]) * q_scale).astype(BF16)

    def scores(unit):
        g, r, off, _ = unit
        cols = slice(g * hd, (g + 1) * hd)
        kb = kn_ref[pl.ds(off, tk), cols]
        ckb = ck_ref[0, g, :, pl.ds(off, tk)] * LOG2_E
        return lax.dot_general(qn_ref[r * rs:(r + 1) * rs, cols], kb, NT_DIMS, preferred_element_type=F32) - ckb

    def softmax_update(unit, s):
        g, r, _, diag_shift = unit
        rows = slice(r * rs, (r + 1) * rs)
        if diag_shift is not None:
            visible = (lax.broadcasted_iota(jnp.int32, (rs, tk), 1) + diag_shift
                       <= lax.broadcasted_iota(jnp.int32, (rs, tk), 0))
            s = jnp.where(visible, s, -jnp.inf)
        m_prev = m_ref[g, rows]
        m_new = jnp.maximum(m_prev, jnp.max(s, axis=-1, keepdims=True))
        alpha = jnp.exp2(m_prev - m_new)
        p = jnp.exp2(s - jnp.concatenate([m_new] * (tk // LANES), axis=1))
        l_ref[g, rows] = alpha * l_ref[g, rows] + jnp.sum(p, axis=-1, keepdims=True)
        m_ref[g, rows] = m_new
        return p.astype(BF16), alpha

    def weighted_values(unit, p, alpha):
        g, r, off, _ = unit
        rows = slice(r * rs, (r + 1) * rs)
        vb = v_ref[pl.ds(off, tk), g * hd:(g + 1) * hd]
        acc_ref[g, rows] = alpha * acc_ref[g, rows] + _dot(p, vb)

    def run(units, lookahead=1):
        s = {u: scores(units[u]) for u in range(min(lookahead, len(units)))}
        for u, unit in enumerate(units):
            p, alpha = softmax_update(unit, s.pop(u))
            if u + lookahead < len(units):
                s[u + lookahead] = scores(units[u + lookahead])
            weighted_values(unit, p, alpha)

    def body(j, carry):
        off = pl.multiple_of(j * tk, tk)
        run([(g, r, off, None) for g in range(heads) for r in range(tq // rs)])
        return carry

    lax.fori_loop(0, qi * (tq // tk), body, 0)

    diag_units = []
    for c in range(tq // tk):
        for g in range(heads):
            for r in range(tq // rs):
                first_key, first_row = c * tk, r * rs
                if first_key > first_row + rs - 1:
                    continue
                unmasked = first_key + tk - 1 <= first_row
                diag_units.append((g, r, pl.multiple_of(qi * tq + first_key, tk),
                                   None if unmasked else first_key - first_row))
    run(diag_units)

    for g in range(heads):
        cols = slice(g * hd, (g + 1) * hd)
        og = og_ref[:, cols].astype(F32)
        o_ref[:, cols] = ((acc_ref[g] / l_ref[g]) * jax.nn.sigmoid(og)).astype(o_ref.dtype)


def _fox_attention(proj, ck, qg, kg, *, bsz, seq, tq=512, heads=4, rs=128, tk=256):
    m = proj.shape[0]
    hd = qg.shape[1]
    assert hd == LANES, "softmax statistics are kept lane-replicated at the head width"
    nq = seq // tq
    groups = FOX_HEADS // heads
    w = heads * hd
    return pl.pallas_call(
        functools.partial(_fox_attn_kernel, heads=heads, hd=hd, rs=rs, tk=tk),
        grid=(bsz, groups, nq),
        in_specs=[
            pl.BlockSpec((tq, w), lambda b, h, i: (b * nq + i, h)),
            pl.BlockSpec((seq, w), lambda b, h, i: (b, groups + h)),
            pl.BlockSpec((seq, w), lambda b, h, i: (b, 2 * groups + h)),
            pl.BlockSpec((tq, w), lambda b, h, i: (b * nq + i, 3 * groups + h)),
            pl.BlockSpec((1, heads, 1, seq), lambda b, h, i: (b, h, 0, 0)),
            pl.BlockSpec((1, hd), lambda b, h, i: (0, 0)),
            pl.BlockSpec((1, hd), lambda b, h, i: (0, 0)),
        ],
        out_specs=pl.BlockSpec((tq, w), lambda b, h, i: (b * nq + i, h)),
        out_shape=jax.ShapeDtypeStruct((m, FOX_HEADS * hd), BF16),
        scratch_shapes=[
            pltpu.VMEM((seq, w), BF16),
            pltpu.VMEM((tq, w), BF16),
            pltpu.VMEM((heads, tq, LANES), F32),
            pltpu.VMEM((heads, tq, LANES), F32),
            pltpu.VMEM((heads, tq, hd), F32),
        ],
        compiler_params=_params("parallel", "parallel", "arbitrary"),
        name="fox_attention",
    )(proj, proj, proj, proj, ck, qg, kg)


def _ffn_kernel(x_ref, gain_ref, sc_ref, sh_ref, g_ref, wg_ref, wv_ref, cwg_ref, cwv_ref, cbg_ref, cbv_ref,
                wd_ref, *rest, tiles_per_seq, final_norm):
    if final_norm:
        fin_ref, o_ref, h_ref, carry_ref, u_ref, act_ref = rest
    else:
        o_ref, h_ref, carry_ref, u_ref, act_ref = rest
    i = pl.program_id(0)
    j = pl.program_id(1)
    tm = x_ref.shape[0]

    @pl.when(j == 0)
    def _():
        o_ref[...] = jnp.zeros_like(o_ref)
        for r0 in range(0, tm, NORM_ROW_CHUNK):
            rows = slice(r0, r0 + NORM_ROW_CHUNK)
            h = _rms_scale(x_ref[rows]) * gain_ref[...] * (1.0 + sc_ref[0]) + sh_ref[0]
            h_ref[rows] = h.astype(BF16)

    @pl.when(i % tiles_per_seq == 0)
    def _():
        carry_ref[j] = jnp.zeros(carry_ref.shape[1:], F32)

    h = h_ref[...]

    def conv(k, r0, cols, cw_ref, cb_ref):
        out = cb_ref[:, cols]
        for s in range(CONV_WIDTH):
            rows = slice(SUBLANES - s + r0, SUBLANES - s + r0 + FFN_ROW_CHUNK)
            out = out + u_ref[k, rows, cols] * cw_ref[CONV_WIDTH - 1 - s:CONV_WIDTH - s, cols]
        return out

    col_blocks = [slice(c, c + MXU_COLS) for c in range(0, wg_ref.shape[1], MXU_COLS)]
    for cols in col_blocks:
        for k, w_ref in enumerate((wg_ref, wv_ref)):
            u_ref[k, :SUBLANES, cols] = carry_ref[j, k, :, cols]
            u_ref[k, SUBLANES:, cols] = _dot(h, w_ref[:, cols])
            carry_ref[j, k, :, cols] = u_ref[k, tm:, cols]
        for r0 in range(0, tm, FFN_ROW_CHUNK):
            gate = conv(0, r0, cols, cwg_ref, cbg_ref)
            val = conv(1, r0, cols, cwv_ref, cbv_ref)
            act_ref[r0:r0 + FFN_ROW_CHUNK, cols] = ((gate * jax.nn.sigmoid(gate)) * val).astype(BF16)
    for cols in col_blocks:
        o_ref[...] += _dot(act_ref[:, cols], wd_ref[cols, :])

    @pl.when(j == pl.num_programs(1) - 1)
    def _():
        for r0 in range(0, tm, NORM_ROW_CHUNK):
            rows = slice(r0, r0 + NORM_ROW_CHUNK)
            y = x_ref[rows] + (1.0 + g_ref[0]) * o_ref[rows]
            if final_norm:
                y = _rms_scale(y) * fin_ref[...]
            o_ref[rows] = y


def _conv_ffn(x, gain, sc, sh, g, w_up, conv_w, conv_b, w_down, final_gain, *, layer, seq, tm=1024, tf=512):
    m, d = x.shape
    d_ff = w_down.shape[0]
    nj = d_ff // tf
    tiles_per_seq = seq // tm
    final_norm = final_gain is not None

    def per_seq(i, j):
        return (i // tiles_per_seq, 0, 0)

    in_specs = [
        pl.BlockSpec((tm, d), lambda i, j: (i, 0), pipeline_mode=pl.Buffered(1)),
        pl.BlockSpec((1, d), lambda i, j: (0, 0)),
        pl.BlockSpec((1, 1, d), per_seq),
        pl.BlockSpec((1, 1, d), per_seq),
        pl.BlockSpec((1, 1, d), per_seq),
        pl.BlockSpec((d, tf), lambda i, j: (0, j)),
        pl.BlockSpec((d, tf), lambda i, j: (0, nj + j)),
        pl.BlockSpec((None, CONV_WIDTH, tf), lambda i, j: (layer, 0, j)),
        pl.BlockSpec((None, CONV_WIDTH, tf), lambda i, j: (layer, 0, nj + j)),
        pl.BlockSpec((None, 1, tf), lambda i, j: (layer, 0, j)),
        pl.BlockSpec((None, 1, tf), lambda i, j: (layer, 0, nj + j)),
        pl.BlockSpec((tf, d), lambda i, j: (j, 0)),
    ]
    args = [x, gain, sc, sh, g, w_up, w_up, conv_w, conv_w, conv_b, conv_b, w_down]
    if final_norm:
        in_specs.append(pl.BlockSpec((1, d), lambda i, j: (0, 0)))
        args.append(final_gain)
    return pl.pallas_call(
        functools.partial(_ffn_kernel, tiles_per_seq=tiles_per_seq, final_norm=final_norm),
        grid=(m // tm, nj),
        in_specs=in_specs,
        out_specs=pl.BlockSpec((tm, d), lambda i, j: (i, 0), pipeline_mode=pl.Buffered(1)),
        out_shape=jax.ShapeDtypeStruct((m, d), F32),
        scratch_shapes=[
            pltpu.VMEM((tm, d), BF16),
            pltpu.VMEM((nj, 2, SUBLANES, tf), F32),
            pltpu.VMEM((2, SUBLANES + tm, tf), F32),
            pltpu.VMEM((tm, tf), BF16),
        ],
        compiler_params=_params("arbitrary", "arbitrary"),
        name="conv_ffn",
    )(*args)


def _pad_cols(w, width):
    return jnp.pad(w, ((0, 0), (0, width - w.shape[1])))


def kernel(x, c, w_mod, b_mod, norm_mix, norm_ffn, gla_w_in, gla_w_gate, gla_b_gate, gla_norm, gla_w_out,
           fox_w_in, fox_b_f, fox_q_norm, fox_k_norm, fox_w_out, ffn_w_up, ffn_conv_w, ffn_conv_b,
           ffn_w_down, norm_final):
    bsz, seq, d = x.shape
    depth = w_mod.shape[0]
    m = bsz * seq
    xf = x.reshape(m, d)

    mod = _modulation(c, w_mod, b_mod).reshape(depth, bsz, 6, 1, d)

    gla_w_in16, fox_w_in16 = gla_w_in.astype(BF16), fox_w_in.astype(BF16)
    conv_b = ffn_conv_b[:, None, :]

    for i in range(depth):
        sh_m, sc_m, g_m, sh_f, sc_f, g_f = (mod[i, :, t] for t in range(6))
        j = i // 2
        if i % 2 == 0:
            n_main = gla_w_in.shape[2] - GLA_RANK
            proj, a, (w_up16, w_down16, w_out16) = _norm_proj(
                xf, norm_mix[i][None], sc_m, sh_m, gla_w_in16[j], _pad_cols(gla_w_in16[j, :, n_main:], LANES),
                [(ffn_w_up, i), (ffn_w_down, i), (gla_w_out, j)], n=n_main, seq=seq)
            wg = jnp.pad(gla_w_gate[j], ((0, LANES - GLA_RANK), (0, 0))).astype(BF16)
            o = _gla_core(proj, a, wg, gla_b_gate[j][None], gla_norm[j][None], bsz=bsz, seq=seq)
        else:
            n_main = fox_w_in.shape[2] - FOX_HEADS
            proj, fl, (w_up16, w_down16, w_out16) = _norm_proj(
                xf, norm_mix[i][None], sc_m, sh_m, fox_w_in16[j], _pad_cols(fox_w_in16[j, :, n_main:], LANES),
                [(ffn_w_up, i), (ffn_w_down, i), (fox_w_out, j)], n=n_main, seq=seq)
            ck = _fox_cum(fl, _pad_cols(fox_b_f[j][None], LANES), bsz=bsz, seq=seq)
            o = _fox_attention(proj, ck.reshape(bsz, FOX_HEADS, 1, seq), fox_q_norm[j][None],
                               fox_k_norm[j][None], bsz=bsz, seq=seq)
        xf = _proj_residual(o, w_out16, xf, g_m, seq=seq)
        xf = _conv_ffn(xf, norm_ffn[i][None], sc_f, sh_f, g_f, w_up16, ffn_conv_w, conv_b, w_down16,
                       norm_final[None] if i == depth - 1 else None, layer=i, seq=seq)
    return xf.reshape(bsz, seq, d)
```
